```python
import math
import jax
import jax.numpy as jnp
from jax import lax
import numpy as np

D_MODEL = 2048
BATCH = 2
SEQ = 8192
DEPTH = 2

GRID_W = 64
CTX_LEN = 256
EPS = 1e-6
F32 = jnp.float32

ATT_HEADS = 8
NOPE_DIM = 128
ROPE_DIM = 64
V_DIM = 128
QK_DIM = NOPE_DIM + ROPE_DIM
Q_LORA = 768
KV_LORA = 512
ROPE_BASE = 10000.0
Q_BLOCK = 128
ATT_SCALE = 1.0 / math.sqrt(QK_DIM)
ATT_WIDTH = ATT_HEADS * V_DIM

SGU_GROUPS = 4
SGU_CHUNK = 128
SGU_WIDTH = 512

SSD_HEADS = 8
SSD_HEAD_DIM = 64
SSD_INNER = SSD_HEADS * SSD_HEAD_DIM
SSD_STATE = 128
SSD_GROUPS = 2
SSD_CONV = 3
SSD_CHUNK = 128
XBC_WIDTH = SSD_INNER + 2 * SSD_GROUPS * SSD_STATE

MIX_WIDTH = ATT_WIDTH + SGU_WIDTH + SSD_INNER

FFN_DIM = 5632
FFN_CONV = 3

OFF_CKV = Q_LORA
OFF_KR = OFF_CKV + KV_LORA
OFF_SGU = OFF_KR + ROPE_DIM
OFF_Z = OFF_SGU + 2 * SGU_WIDTH
OFF_XBC = OFF_Z + SSD_INNER
OFF_DT = OFF_XBC + XBC_WIDTH
P_IN = OFF_DT + 2 * SSD_HEADS

kernel_name = 'hybrid_mla_sgu_ssd_prefix_dit_block'


def rms_norm(t, g):
    tf = t.astype(F32)
    y = tf * lax.rsqrt(jnp.mean(tf * tf, axis=-1, keepdims=True) + EPS)
    return (y * g.astype(F32)).astype(t.dtype)


def modulate(h, shift, scale):
    return h * (1.0 + scale) + shift


def dwconv_centred(t, w, b):
    k = w.shape[0]
    pad = k // 2
    n = t.shape[1]
    tp = jnp.pad(t, ((0, 0), (pad, pad), (0, 0)))
    out = b + w[0] * tp[:, 0:n]
    for i in range(1, k):
        out = out + w[i] * tp[:, i:i + n]
    return out


def axial_rope_tables(rows):
    row = jnp.repeat(jnp.arange(rows), GRID_W)
    col = jnp.tile(jnp.arange(GRID_W), rows)
    pairs = ROPE_DIM // 4
    freqs = ROPE_BASE ** (-jnp.arange(pairs, dtype=F32) / pairs)
    pos = jnp.stack([row, col], axis=-1).astype(F32)
    ang = pos[:, :, None] * freqs
    return jnp.cos(ang), jnp.sin(ang)


def apply_axial_rope(t, cos, sin):
    b, n, h, _ = t.shape
    t_nope = t[..., :NOPE_DIM]
    r = t[..., NOPE_DIM:].reshape(b, n, h, 2, ROPE_DIM // 2).astype(F32)
    half = ROPE_DIM // 4
    r1, r2 = r[..., :half], r[..., half:]
    cs, sn = cos[None, :, None], sin[None, :, None]
    rot = jnp.concatenate([r1 * cs - r2 * sn, r2 * cs + r1 * sn], axis=-1)
    return jnp.concatenate([t_nope, rot.reshape(b, n, h, ROPE_DIM).astype(t.dtype)], axis=-1)


def mla_qkv(p, q_a_norm, w_uq, kv_a_norm, w_ukv, q_norm, k_norm):
    b, n, _ = p.shape
    cq = rms_norm(p[..., :OFF_CKV], q_a_norm)
    ckv = rms_norm(p[..., OFF_CKV:OFF_KR], kv_a_norm)
    kr = p[..., OFF_KR:OFF_SGU]
    q = (cq @ w_uq).reshape(b, n, ATT_HEADS, QK_DIM)
    kv = (ckv @ w_ukv).reshape(b, n, ATT_HEADS, NOPE_DIM + V_DIM)
    k = jnp.concatenate([kv[..., :NOPE_DIM], jnp.broadcast_to(kr[:, :, None, :], (b, n, ATT_HEADS, ROPE_DIM))], axis=-1)
    v = kv[..., NOPE_DIM:]
    return rms_norm(q, q_norm), rms_norm(k, k_norm), v


def block_attention(q, k, v):
    b, s, h, dq = q.shape
    nb = s // Q_BLOCK
    qb = jnp.swapaxes(q.reshape(b, nb, Q_BLOCK, h, dq), 0, 1)

    def one_block(q_blk):
        sc = jnp.einsum('bqhd,bkhd->bhqk', q_blk, k).astype(F32) * ATT_SCALE
        pr = jax.nn.softmax(sc, axis=-1).astype(v.dtype)
        return jnp.einsum('bhqk,bkhd->bqhd', pr, v)

    o = lax.map(one_block, qb)
    return jnp.swapaxes(o, 0, 1).reshape(b, s, h * v.shape[-1])


def spatial_gating(z, norm_g, w_s, b_s):
    b, n, _ = z.shape
    u, v = z[..., :SGU_WIDTH], z[..., SGU_WIDTH:]
    v = rms_norm(v, norm_g).reshape(b, n // SGU_CHUNK, SGU_CHUNK, SGU_GROUPS, SGU_WIDTH // SGU_GROUPS)
    mixed = jnp.einsum('gij,bcjgd->bcigd', w_s, v) + jnp.swapaxes(b_s, 0, 1)[:, :, None]
    return u * mixed.reshape(b, n, SGU_WIDTH)


def ssd_inputs(p, conv_w, conv_b, dt_bias):
    b, n, _ = p.shape
    z = p[..., OFF_Z:OFF_XBC]
    xbc = jax.nn.silu(dwconv_centred(p[..., OFF_XBC:OFF_DT], conv_w, conv_b))
    xs = xbc[..., :SSD_INNER].reshape(b, n, SSD_HEADS, SSD_HEAD_DIM)
    gn = SSD_GROUPS * SSD_STATE
    rep = SSD_HEADS // SSD_GROUPS
    bm = jnp.repeat(xbc[..., SSD_INNER:SSD_INNER + gn].reshape(b, n, SSD_GROUPS, SSD_STATE), rep, axis=2)
    cm = jnp.repeat(xbc[..., SSD_INNER + gn:].reshape(b, n, SSD_GROUPS, SSD_STATE), rep, axis=2)
    dt = jax.nn.softplus(p[..., OFF_DT:P_IN].reshape(b, n, 2, SSD_HEADS).astype(F32) + dt_bias.astype(F32))
    return z, xs, bm, cm, dt


def ssd_chunked(xs, dt, a_neg, bm, cm):
    b, l, h, p = xs.shape
    n = bm.shape[-1]
    q = SSD_CHUNK
    nc = l // q
    acs = jnp.cumsum((dt * a_neg).reshape(b, nc, q, h), axis=2)
    xdt = (xs * dt[..., None]).reshape(b, nc, q, h, p)
    bm = bm.reshape(b, nc, q, h, n)
    cm = cm.reshape(b, nc, q, h, n)
    lower = jnp.tril(jnp.ones((q, q), dtype=bool))
    seg = acs[:, :, :, None, :] - acs[:, :, None, :, :]
    decay = jnp.exp(jnp.where(lower[None, None, :, :, None], seg, -jnp.inf))
    cb = jnp.einsum('bcihn,bcjhn->bcijh', cm, bm)
    y_diag = jnp.einsum('bcijh,bcjhp->bcihp', cb * decay, xdt)
    to_end = jnp.exp(acs[:, :, -1:, :] - acs)
    states = jnp.einsum('bcqhn,bcqh,bcqhp->bchpn', bm, to_end, xdt)
    chunk_decay = jnp.exp(acs[:, :, -1, :])

    def carry(state, inp):
        st, dec = inp
        return state * dec[:, :, None, None] + st, state

    init = jnp.zeros((b, h, p, n), states.dtype)
    _, entering = lax.scan(carry, init, (jnp.moveaxis(states, 1, 0), jnp.moveaxis(chunk_decay, 1, 0)))
    entering = jnp.moveaxis(entering, 0, 1)
    y_off = jnp.einsum('bcqhn,bchpn->bcqhp', cm, entering) * jnp.exp(acs)[..., None]
    return (y_diag + y_off).reshape(b, l, h, p).astype(xs.dtype)


def merge_groups(att, sgu, ssd, attn_out_norm, gmlp_out_norm, w_out):
    cat = jnp.concatenate([rms_norm(att, attn_out_norm), rms_norm(sgu, gmlp_out_norm), ssd], axis=-1)
    return cat @ w_out


def token_mixers(h_lat, h_ctx, need_ctx, cos, sin, w_in, q_a_norm, w_uq, kv_a_norm, w_ukv, q_norm, k_norm,
                 attn_out_norm, sgu_norm, sgu_w, sgu_b, gmlp_out_norm, ssd_conv_w, ssd_conv_b, ssd_dt_bias,
                 ssd_a_log, ssd_d, ssd_norm, w_out):
    b, s, _ = h_lat.shape
    n_ctx = h_ctx.shape[1]
    p_lat = h_lat @ w_in
    p_ctx = h_ctx @ w_in

    q_l, k_l, v_l = mla_qkv(p_lat, q_a_norm, w_uq, kv_a_norm, w_ukv, q_norm, k_norm)
    q_c, k_c, v_c = mla_qkv(p_ctx, q_a_norm, w_uq, kv_a_norm, w_ukv, q_norm, k_norm)
    q_l = apply_axial_rope(q_l, cos, sin)
    k_l = apply_axial_rope(k_l, cos, sin)
    att_l = block_attention(q_l, jnp.concatenate([k_c, k_l], axis=1), jnp.concatenate([v_c, v_l], axis=1))

    sgu_l = spatial_gating(jax.nn.gelu(p_lat[..., OFF_SGU:OFF_Z]), sgu_norm, sgu_w, sgu_b)

    z_l, xs_l, bm_l, cm_l, dt_l = ssd_inputs(p_lat, ssd_conv_w, ssd_conv_b, ssd_dt_bias)
    z_c, xs_c, bm_c, cm_c, dt_c = ssd_inputs(p_ctx, ssd_conv_w, ssd_conv_b, ssd_dt_bias)
    a_neg = -jnp.exp(ssd_a_log.astype(F32))
    cat = lambda u, w: jnp.concatenate([u, w], axis=1)
    flip = lambda t: jnp.flip(t, axis=1)
    y_f = ssd_chunked(cat(xs_c, xs_l), cat(dt_c[:, :, 0], dt_l[:, :, 0]), a_neg[0],
                      cat(bm_c, bm_l), cat(cm_c, cm_l))
    y_b = ssd_chunked(cat(flip(xs_c), flip(xs_l)), cat(flip(dt_c[:, :, 1]), flip(dt_l[:, :, 1])), a_neg[1],
                      cat(flip(bm_c), flip(bm_l)), cat(flip(cm_c), flip(cm_l)))
    y_l = y_f[:, n_ctx:] + flip(y_b[:, n_ctx:]) + ssd_d[:, None] * xs_l
    ssd_l = rms_norm(y_l.reshape(b, s, SSD_INNER) * jax.nn.silu(z_l), ssd_norm)

    mix_l = merge_groups(att_l, sgu_l, ssd_l, attn_out_norm, gmlp_out_norm, w_out)
    if not need_ctx:
        return mix_l, None

    att_c = block_attention(q_c, k_c, v_c)
    sgu_c = spatial_gating(jax.nn.gelu(p_ctx[..., OFF_SGU:OFF_Z]), sgu_norm, sgu_w, sgu_b)
    y_c = y_f[:, :n_ctx] + flip(y_b[:, :n_ctx]) + ssd_d[:, None] * xs_c
    ssd_c = rms_norm(y_c.reshape(b, n_ctx, SSD_INNER) * jax.nn.silu(z_c), ssd_norm)
    mix_c = merge_groups(att_c, sgu_c, ssd_c, attn_out_norm, gmlp_out_norm, w_out)
    return mix_l, mix_c


def conv_ffn(h, w_gate, w_up, conv_w, conv_b, w_down):
    g = dwconv_centred(h @ w_gate, conv_w, conv_b)
    return (jax.nn.silu(g) * (h @ w_up)) @ w_down


def setup_inputs(seed: int = 0) -> dict:
    key = jax.random.key(seed)
    ks = jax.random.split(key, 40)

    def nrm(i, shape, scale):
        return jax.random.normal(ks[i], shape, F32) * scale

    def gain(i, shape):
        return 1.0 + 0.05 * jax.random.normal(ks[i], shape, F32)

    L, D = DEPTH, D_MODEL
    x = nrm(0, (BATCH, SEQ, D), 1.0)
    c = nrm(1, (BATCH, D), 1.0)
    ctx = nrm(2, (BATCH, CTX_LEN, D), 1.0)
    c_ctx = nrm(3, (D,), 1.0)
    w_mod = nrm(4, (L, D, 6 * D), 0.5 * D ** -0.5)
    b_mod = nrm(5, (L, 6 * D), 0.02)
    norm_mix = gain(6, (L, D))
    norm_ffn = gain(7, (L, D))
    w_in = nrm(8, (L, D, P_IN), D ** -0.5)
    q_a_norm = gain(9, (L, Q_LORA))
    w_uq = nrm(10, (L, Q_LORA, ATT_HEADS * QK_DIM), Q_LORA ** -0.5)
    kv_a_norm = gain(11, (L, KV_LORA))
    w_ukv = nrm(12, (L, KV_LORA, ATT_HEADS * (NOPE_DIM + V_DIM)), KV_LORA ** -0.5)
    q_norm = gain(13, (L, QK_DIM))
    k_norm = gain(14, (L, QK_DIM))
    attn_out_norm = gain(15, (L, ATT_WIDTH))
    sgu_norm = gain(16, (L, SGU_WIDTH))
    sgu_w = nrm(17, (L, SGU_GROUPS, SGU_CHUNK, SGU_CHUNK), SGU_CHUNK ** -0.5)
    sgu_b = nrm(18, (L, SGU_GROUPS, SGU_CHUNK), 0.02)
    gmlp_out_norm = gain(19, (L, SGU_WIDTH))
    ssd_conv_w = nrm(20, (L, SSD_CONV, XBC_WIDTH), SSD_CONV ** -0.5)
    ssd_conv_b = nrm(21, (L, XBC_WIDTH), 0.02)
    dt0 = jnp.exp(jax.random.uniform(ks[22], (L, 2, SSD_HEADS), F32, math.log(1e-3), math.log(1e-1)))
    ssd_dt_bias = dt0 + jnp.log(-jnp.expm1(-dt0))
    ssd_a_log = jnp.log(jax.random.uniform(ks[23], (L, 2, SSD_HEADS), F32, 1.0, 16.0))
    ssd_d = gain(24, (L, SSD_HEADS))
    ssd_norm = gain(25, (L, SSD_INNER))
    w_out = nrm(26, (L, MIX_WIDTH, D), MIX_WIDTH ** -0.5)
    ffn_w_gate = nrm(27, (L, D, FFN_DIM), D ** -0.5)
    ffn_w_up = nrm(28, (L, D, FFN_DIM), D ** -0.5)
    ffn_conv_w = nrm(29, (L, FFN_CONV, FFN_DIM), FFN_CONV ** -0.5)
    ffn_conv_b = nrm(30, (L, FFN_DIM), 0.02)
    ffn_w_down = nrm(31, (L, FFN_DIM, D), FFN_DIM ** -0.5)
    return {'x': x, 'c': c, 'ctx': ctx, 'c_ctx': c_ctx, 'w_mod': w_mod, 'b_mod': b_mod,
            'norm_mix': norm_mix, 'norm_ffn': norm_ffn, 'w_in': w_in, 'q_a_norm': q_a_norm, 'w_uq': w_uq,
            'kv_a_norm': kv_a_norm, 'w_ukv': w_ukv, 'q_norm': q_norm, 'k_norm': k_norm,
            'attn_out_norm': attn_out_norm, 'sgu_norm': sgu_norm, 'sgu_w': sgu_w, 'sgu_b': sgu_b,
            'gmlp_out_norm': gmlp_out_norm, 'ssd_conv_w': ssd_conv_w, 'ssd_conv_b': ssd_conv_b,
            'ssd_dt_bias': ssd_dt_bias, 'ssd_a_log': ssd_a_log, 'ssd_d': ssd_d, 'ssd_norm': ssd_norm,
            'w_out': w_out, 'ffn_w_gate': ffn_w_gate, 'ffn_w_up': ffn_w_up, 'ffn_conv_w': ffn_conv_w,
            'ffn_conv_b': ffn_conv_b, 'ffn_w_down': ffn_w_down}


def reference(x, c, ctx, c_ctx, w_mod, b_mod, norm_mix, norm_ffn, w_in, q_a_norm, w_uq, kv_a_norm, w_ukv,
              q_norm, k_norm, attn_out_norm, sgu_norm, sgu_w, sgu_b, gmlp_out_norm, ssd_conv_w, ssd_conv_b,
              ssd_dt_bias, ssd_a_log, ssd_d, ssd_norm, w_out, ffn_w_gate, ffn_w_up, ffn_conv_w, ffn_conv_b,
              ffn_w_down):
    rows = x.shape[1] // GRID_W
    cos, sin = axial_rope_tables(rows)
    x_lat, x_ctx = x, ctx
    for i in range(DEPTH):
        need_ctx = i < DEPTH - 1
        mods_l = jnp.split(jax.nn.silu(c) @ w_mod[i] + b_mod[i], 6, axis=-1)
        sh1, sc1, g1, sh2, sc2, g2 = [m[:, None, :] for m in mods_l]
        csh1, csc1, cg1, csh2, csc2, cg2 = jnp.split(jax.nn.silu(c_ctx) @ w_mod[i] + b_mod[i], 6, axis=-1)

        h_lat = modulate(rms_norm(x_lat, norm_mix[i]), sh1, sc1)
        h_ctx = modulate(rms_norm(x_ctx, norm_mix[i]), csh1, csc1)
        mix_lat, mix_ctx = token_mixers(
            h_lat, h_ctx, need_ctx, cos, sin, w_in[i], q_a_norm[i], w_uq[i], kv_a_norm[i], w_ukv[i],
            q_norm[i], k_norm[i], attn_out_norm[i], sgu_norm[i], sgu_w[i], sgu_b[i], gmlp_out_norm[i],
            ssd_conv_w[i], ssd_conv_b[i], ssd_dt_bias[i], ssd_a_log[i], ssd_d[i], ssd_norm[i], w_out[i])
        x_lat = x_lat + g1 * mix_lat
        x_lat = x_lat + g2 * conv_ffn(modulate(rms_norm(x_lat, norm_ffn[i]), sh2, sc2),
                                      ffn_w_gate[i], ffn_w_up[i], ffn_conv_w[i], ffn_conv_b[i], ffn_w_down[i])
        if need_ctx:
            x_ctx = x_ctx + cg1 * mix_ctx
            x_ctx = x_ctx + cg2 * conv_ffn(modulate(rms_norm(x_ctx, norm_ffn[i]), csh2, csc2),
                                           ffn_w_gate[i], ffn_w_up[i], ffn_conv_w[i], ffn_conv_b[i], ffn_w_down[i])
    return x_lat
```

```python
import functools
import math

import jax
import jax.numpy as jnp
import numpy as np
from jax import lax
from jax.experimental import pallas as pl
from jax.experimental.pallas import tpu as pltpu

F32 = jnp.float32
BF16 = jnp.bfloat16
NORM_EPS = 1e-6

GRID_COLS = 64
N_HEADS = 8
NOPE = 128
ROPE = 64
QK = NOPE + ROPE
V_DIM = 128
Q_LORA = 768
KV_LORA = 512
ROPE_THETA = 10000.0
SM_SCALE = 1.0 / math.sqrt(QK)
HEAD_PAD = 256
ATT_W = N_HEADS * V_DIM
SGU_G = 4
SGU_CH = 128
SGU_W = 512
SSD_H = 8
SSD_P = 64
SSD_IN = SSD_H * SSD_P
SSD_N = 128
SSD_G = 2
SSD_Q = 128
XBC_W = SSD_IN + 2 * SSD_G * SSD_N
GRP_W = (SSD_H // SSD_G) * SSD_P

V7X_LANES = 128
V7X_SUBLANES = 8
V7X_VMEM_BYTES = 64 * 1024 * 1024
VMEM_LIMIT = 56 * 1024 * 1024

SEG_Q = (0, Q_LORA)
SEG_KV = (SEG_Q[1], SEG_Q[1] + KV_LORA + 2 * ROPE)
SEG_SGU = (SEG_KV[1], SEG_KV[1] + 2 * SGU_W)
SEG_Z = (SEG_SGU[1], SEG_SGU[1] + SSD_IN)
SEG_XBC = (SEG_Z[1], SEG_Z[1] + XBC_W)
SEG_DT = (SEG_XBC[1], SEG_XBC[1] + V7X_LANES)
P_PAD = SEG_DT[1]

_ROT_PERM = np.concatenate([np.arange(16, 32), np.arange(0, 16), np.arange(48, 64), np.arange(32, 48)])


def _params(sem, vmem=VMEM_LIMIT):
    return pltpu.CompilerParams(dimension_semantics=sem, vmem_limit_bytes=vmem)


def _dot(a, b):
    return jnp.dot(a, b, preferred_element_type=F32)


def _dot_nt(a, b):
    return lax.dot_general(a, b, (((1,), (1,)), ((), ())), preferred_element_type=F32)


def _rms(t, gain):
    return t * lax.rsqrt(jnp.mean(t * t, axis=-1, keepdims=True) + NORM_EPS) * gain


def _silu(t):
    return t * (1.0 / (1.0 + jnp.exp(-t)))


def _gelu_tanh(t):
    return 0.5 * t * (1.0 + jnp.tanh(math.sqrt(2.0 / math.pi) * (t + 0.044715 * (t * t * t))))


def _softplus(t):
    return jnp.maximum(t, 0.0) + jnp.log1p(jnp.exp(-jnp.abs(t)))


def _split3(t):
    hi = t.astype(BF16)
    r1 = t - hi.astype(F32)
    mid = r1.astype(BF16)
    lo = (r1 - mid.astype(F32)).astype(BF16)
    return hi, mid, lo


def _mods_kernel(c_ref, w_ref, b_ref, o_ref):
    s = _silu(c_ref[...])
    s_hi = s.astype(BF16)
    s_lo = (s - s_hi.astype(F32)).astype(BF16)
    w = w_ref[0]
    w_hi = w.astype(BF16)
    w_lo = (w - w_hi.astype(F32)).astype(BF16)
    o_ref[0] = _dot(s_hi, w_hi) + _dot(s_lo, w_hi) + _dot(s_hi, w_lo) + b_ref[0]


def _mods(cc, w_mod, b_mod):
    depth, d, n6 = w_mod.shape
    tn = 1024
    return pl.pallas_call(
        _mods_kernel,
        out_shape=jax.ShapeDtypeStruct((depth, V7X_SUBLANES, n6), F32),
        grid=(depth, n6 // tn),
        in_specs=[pl.BlockSpec((V7X_SUBLANES, d), lambda l, j: (0, 0)),
                  pl.BlockSpec((1, d, tn), lambda l, j: (l, 0, j)),
                  pl.BlockSpec((1, 1, tn), lambda l, j: (l, 0, j))],
        out_specs=pl.BlockSpec((1, V7X_SUBLANES, tn), lambda l, j: (l, 0, j)),
        compiler_params=_params(("parallel", "parallel")),
        name="mods",
    )(cc, w_mod, b_mod.reshape(depth, 1, n6))


def _in_proj_kernel(x_ref, mod_ref, g_ref, w_ref, q_ref, kv_ref, sgu_ref, z_ref, xbc_ref, dt_ref):
    x = x_ref[0]
    shift = mod_ref[0, 0:1, :]
    scale = mod_ref[0, 1:2, :]
    h = (_rms(x, g_ref[...]) * (1.0 + scale) + shift).astype(BF16)
    for ref, (a, b) in ((q_ref, SEG_Q), (kv_ref, SEG_KV), (sgu_ref, SEG_SGU), (z_ref, SEG_Z),
                        (xbc_ref, SEG_XBC), (dt_ref, SEG_DT)):
        ref[0] = _dot(h, w_ref[:, a:b]).astype(ref.dtype)


def _in_proj(x, mod, gain, w_in_p, tm):
    b, s, d = x.shape
    segs = (SEG_Q, SEG_KV, SEG_SGU, SEG_Z, SEG_XBC, SEG_DT)
    dts = (BF16, BF16, BF16, BF16, BF16, F32)
    return pl.pallas_call(
        _in_proj_kernel,
        out_shape=[jax.ShapeDtypeStruct((b, s, hi - lo), dt) for (lo, hi), dt in zip(segs, dts)],
        grid=(b, s // tm),
        in_specs=[pl.BlockSpec((1, tm, d), lambda bi, i: (bi, i, 0)),
                  pl.BlockSpec((1, V7X_SUBLANES, d), lambda bi, i: (bi, 0, 0)),
                  pl.BlockSpec((1, d), lambda bi, i: (0, 0)),
                  pl.BlockSpec((d, P_PAD), lambda bi, i: (0, 0), pipeline_mode=pl.Buffered(1))],
        out_specs=[pl.BlockSpec((1, tm, hi - lo), lambda bi, i: (bi, i, 0)) for lo, hi in segs],
        compiler_params=_params(("parallel", "parallel")),
        name="in_proj",
    )(x, mod, gain, w_in_p)


def _rotary_half(r, gain2, cs, rs):
    zr = r * gain2 * cs
    lane = lax.broadcasted_iota(jnp.int32, zr.shape, 1)
    return jnp.where(lane < ROPE, (zr + pltpu.roll(zr, ROPE, axis=1)) * rs, 0.0)


def _q_proj_kernel(cq_ref, an_ref, w_ref, g_ref, g2_ref, cs_ref, q_ref):
    cq = cq_ref[0].astype(F32)
    cqn = _rms(cq, an_ref[...]).astype(BF16)
    y = _dot(cqn, w_ref[...])
    cs = cs_ref[...]
    lane = lax.broadcasted_iota(jnp.int32, cs.shape, 1)
    for h in range(N_HEADS):
        a = y[:, h * HEAD_PAD:h * HEAD_PAD + NOPE]
        r = y[:, h * HEAD_PAD + NOPE:(h + 1) * HEAD_PAD]
        ssq = jnp.sum(a * a, axis=-1, keepdims=True) + jnp.sum(jnp.where(lane < ROPE, r * r, 0.0), axis=-1, keepdims=True)
        rs = lax.rsqrt(ssq * (1.0 / QK) + NORM_EPS) * SM_SCALE
        q_ref[0, :, h * HEAD_PAD:h * HEAD_PAD + NOPE] = (a * g_ref[...] * rs).astype(BF16)
        q_ref[0, :, h * HEAD_PAD + NOPE:(h + 1) * HEAD_PAD] = _rotary_half(r, g2_ref[...], cs, rs).astype(BF16)


def _q_proj(cq, an, w_uq_p, g, g2, cs, tm):
    b, s, _ = cq.shape
    width = N_HEADS * HEAD_PAD
    return pl.pallas_call(
        _q_proj_kernel,
        out_shape=jax.ShapeDtypeStruct((b, s, width), BF16),
        grid=(b, s // tm),
        in_specs=[pl.BlockSpec((1, tm, Q_LORA), lambda bi, i: (bi, i, 0)),
                  pl.BlockSpec((1, Q_LORA), lambda bi, i: (0, 0)),
                  pl.BlockSpec((Q_LORA, width), lambda bi, i: (0, 0)),
                  pl.BlockSpec((1, NOPE), lambda bi, i: (0, 0)),
                  pl.BlockSpec((1, 2 * ROPE), lambda bi, i: (0, 0)),
                  pl.BlockSpec((tm, 2 * ROPE), lambda bi, i: (i, 0))],
        out_specs=pl.BlockSpec((1, tm, width), lambda bi, i: (bi, i, 0)),
        compiler_params=_params(("parallel", "parallel")),
        name="q_proj",
    )(cq, an, w_uq_p, g, g2, cs)


def _kv_proj_kernel(t_ref, an_ref, w_ref, g_ref, g2_ref, cs_ref, k_ref, v_ref):
    ckv = t_ref[0, :, :KV_LORA].astype(F32)
    kr2 = t_ref[0, :, KV_LORA:].astype(F32)
    ckvn = _rms(ckv, an_ref[...]).astype(BF16)
    y = _dot(ckvn, w_ref[...])
    lane = lax.broadcasted_iota(jnp.int32, kr2.shape, 1)
    kr_ssq = jnp.sum(jnp.where(lane < ROPE, kr2 * kr2, 0.0), axis=-1, keepdims=True)
    rot = _rotary_half(kr2, g2_ref[...], cs_ref[...], 1.0)
    for h in range(N_HEADS):
        kn = y[:, h * 2 * NOPE:h * 2 * NOPE + NOPE]
        rs = lax.rsqrt((jnp.sum(kn * kn, axis=-1, keepdims=True) + kr_ssq) * (1.0 / QK) + NORM_EPS)
        k_ref[0, :, h * HEAD_PAD:h * HEAD_PAD + NOPE] = (kn * g_ref[...] * rs).astype(BF16)
        k_ref[0, :, h * HEAD_PAD + NOPE:(h + 1) * HEAD_PAD] = (rot * rs).astype(BF16)
        v_ref[0, :, h * V_DIM:(h + 1) * V_DIM] = y[:, h * 2 * NOPE + NOPE:(h + 1) * 2 * NOPE].astype(BF16)


def _kv_proj(t, an, w_ukv, g, g2, cs, tm):
    b, s, wt = t.shape
    return pl.pallas_call(
        _kv_proj_kernel,
        out_shape=[jax.ShapeDtypeStruct((b, s, N_HEADS * HEAD_PAD), BF16),
                   jax.ShapeDtypeStruct((b, s, ATT_W), BF16)],
        grid=(b, s // tm),
        in_specs=[pl.BlockSpec((1, tm, wt), lambda bi, i: (bi, i, 0)),
                  pl.BlockSpec((1, KV_LORA), lambda bi, i: (0, 0)),
                  pl.BlockSpec((KV_LORA, N_HEADS * 2 * NOPE), lambda bi, i: (0, 0)),
                  pl.BlockSpec((1, NOPE), lambda bi, i: (0, 0)),
                  pl.BlockSpec((1, 2 * ROPE), lambda bi, i: (0, 0)),
                  pl.BlockSpec((tm, 2 * ROPE), lambda bi, i: (i, 0))],
        out_specs=[pl.BlockSpec((1, tm, N_HEADS * HEAD_PAD), lambda bi, i: (bi, i, 0)),
                   pl.BlockSpec((1, tm, ATT_W), lambda bi, i: (bi, i, 0))],
        compiler_params=_params(("parallel", "parallel")),
        name="kv_proj",
    )(t, an, w_ukv, g, g2, cs)


def _attn_kernel(*refs, n_lat, tk):
    if n_lat:
        q_ref, kc_ref, vc_ref, kl_ref, vl_ref, o_ref = refs
    else:
        q_ref, kc_ref, vc_ref, o_ref = refs
    q = q_ref[0]
    tq = q.shape[0]

    def step(carry, k, v):
        m, l, acc = carry
        s = _dot_nt(q, k)
        m_new = jnp.maximum(m, jnp.max(s, axis=-1, keepdims=True))
        p = jnp.exp(s - m_new)
        alpha = jnp.exp(m - m_new)
        l = alpha * l + jnp.sum(p, axis=-1, keepdims=True)
        acc = alpha * acc + _dot(p.astype(BF16), v)
        return m_new, l, acc

    carry = (jnp.full((tq, 1), -jnp.inf, F32), jnp.zeros((tq, 1), F32), jnp.zeros((tq, V_DIM), F32))
    carry = step(carry, kc_ref[0], vc_ref[0])
    if n_lat:
        def body(j, c):
            start = pl.multiple_of(j * tk, tk)
            return step(c, kl_ref[0, pl.ds(start, tk), :], vl_ref[0, pl.ds(start, tk), :])
        carry = lax.fori_loop(0, n_lat // tk, body, carry)
    _, l, acc = carry
    o_ref[0] = (acc * (1.0 / l)).astype(o_ref.dtype)


def _attention(q, k_c, v_c, k_l=None, v_l=None, *, tq, tk=512):
    b, s, _ = q.shape
    n_ctx = k_c.shape[1]
    n_lat = 0 if k_l is None else k_l.shape[1]
    in_specs = [pl.BlockSpec((1, tq, HEAD_PAD), lambda bi, h, i: (bi, i, h)),
                pl.BlockSpec((1, n_ctx, HEAD_PAD), lambda bi, h, i: (bi, 0, h)),
                pl.BlockSpec((1, n_ctx, V_DIM), lambda bi, h, i: (bi, 0, h))]
    args = [q, k_c, v_c]
    if n_lat:
        in_specs += [pl.BlockSpec((1, n_lat, HEAD_PAD), lambda bi, h, i: (bi, 0, h)),
                     pl.BlockSpec((1, n_lat, V_DIM), lambda bi, h, i: (bi, 0, h))]
        args += [k_l, v_l]
    return pl.pallas_call(
        functools.partial(_attn_kernel, n_lat=n_lat, tk=tk),
        out_shape=jax.ShapeDtypeStruct((b, s, ATT_W), BF16),
        grid=(b, N_HEADS, s // tq),
        in_specs=in_specs,
        out_specs=pl.BlockSpec((1, tq, V_DIM), lambda bi, h, i: (bi, i, h)),
        compiler_params=_params(("parallel", "parallel", "arbitrary")),
        name="attention",
    )(*args)


def _sgu_kernel(p_ref, gn_ref, w_ref, b_ref, o_ref, *, n_chunks):
    for c in range(n_chunks):
        rows = slice(c * SGU_CH, (c + 1) * SGU_CH)
        z = _gelu_tanh(p_ref[0, rows, :].astype(F32))
        u = z[:, :SGU_W]
        vn = _rms(z[:, SGU_W:], gn_ref[...]).astype(BF16)
        for g in range(SGU_G):
            cols = slice(g * SGU_CH, (g + 1) * SGU_CH)
            mixed = _dot(w_ref[g], vn[:, cols]) + b_ref[g]
            o_ref[0, rows, cols] = (u[:, cols] * mixed).astype(o_ref.dtype)


def _sgu(p, gn, w_s, b_b, rows):
    b, s, _ = p.shape
    return pl.pallas_call(
        functools.partial(_sgu_kernel, n_chunks=rows // SGU_CH),
        out_shape=jax.ShapeDtypeStruct((b, s, SGU_W), BF16),
        grid=(b, s // rows),
        in_specs=[pl.BlockSpec((1, rows, 2 * SGU_W), lambda bi, i: (bi, i, 0)),
                  pl.BlockSpec((1, SGU_W), lambda bi, i: (0, 0)),
                  pl.BlockSpec((SGU_G, SGU_CH, SGU_CH), lambda bi, i: (0, 0, 0)),
                  pl.BlockSpec((SGU_G, SGU_CH, SGU_CH), lambda bi, i: (0, 0, 0))],
        out_specs=pl.BlockSpec((1, rows, SGU_W), lambda bi, i: (bi, i, 0)),
        compiler_params=_params(("parallel", "parallel")),
        name="sgu",
    )(p, gn, w_s, b_b)


def _ssd_direction(xm_ref, xp_ref, xn_ref, dt_ref, chunk, n_chunks, cw_ref, cb_ref, bias_ref, alog_ref,
                   tri_ref, exp_ref, state_ref, y_ref, xs_ref, *, reverse):
    q = SSD_Q
    x = xm_ref[0].astype(F32)
    row = lax.broadcasted_iota(jnp.int32, x.shape, 0)
    prev_row = jnp.where(chunk > 0, xp_ref[0, V7X_SUBLANES - 1:V7X_SUBLANES, :].astype(F32), 0.0)
    next_row = jnp.where(chunk < n_chunks - 1, xn_ref[0, 0:1, :].astype(F32), 0.0)
    up = jnp.where(row == 0, prev_row, pltpu.roll(x, 1, axis=0))
    dn = jnp.where(row == q - 1, next_row, pltpu.roll(x, q - 1, axis=0))
    xbc = _silu(cb_ref[...] + cw_ref[0:1, :] * up + cw_ref[1:2, :] * x + cw_ref[2:3, :] * dn)
    xs = xbc[:, :SSD_IN]
    if xs_ref is not None:
        xs_ref[0] = xs.astype(xs_ref.dtype)

    dt = _softplus(dt_ref[0] + bias_ref[...])
    a_dt = dt * (-jnp.exp(alog_ref[...]))
    tri = tri_ref[...]
    acs = sum(_dot(tri, part) for part in _split3(a_dt))
    acs_t = acs.T
    expand = exp_ref[...]
    acs_x = sum(_dot(part, expand) for part in _split3(acs))
    dt_hi = dt.astype(BF16)
    dt_x = _dot(dt_hi, expand) + _dot((dt - dt_hi.astype(F32)).astype(BF16), expand)
    last = acs_x[0:1, :] if reverse else acs_x[q - 1:q, :]
    xdt = xs * dt_x
    xdt_b = xdt.astype(BF16)
    x_end = (xdt * jnp.exp(last - acs_x)).astype(BF16)
    decay_out = jnp.exp(acs_x)
    chunk_decay = jnp.exp(last)

    ti = lax.broadcasted_iota(jnp.int32, (q, q), 0)
    tj = lax.broadcasted_iota(jnp.int32, (q, q), 1)
    keep = (ti <= tj) if reverse else (ti >= tj)
    lane_blk = lax.shift_right_logical(lax.broadcasted_iota(jnp.int32, (q, GRP_W), 1), int(math.log2(SSD_P)))
    head0 = SSD_H if reverse else 0
    for g in range(SSD_G):
        bm = xbc[:, SSD_IN + g * SSD_N:SSD_IN + (g + 1) * SSD_N]
        cm = xbc[:, SSD_IN + (SSD_G + g) * SSD_N:SSD_IN + (SSD_G + g + 1) * SSD_N].astype(BF16)
        cb = _dot_nt(cm, bm.astype(BF16))
        cols = slice(g * GRP_W, (g + 1) * GRP_W)
        state = state_ref[g]
        y = _dot(cm, state.astype(BF16)) * decay_out[:, cols]
        state_ref[g] = state * chunk_decay[:, cols] + _dot(bm.T.astype(BF16), x_end[:, cols])
        for hh in range(SSD_H // SSD_G):
            hcol = head0 + g * (SSD_H // SSD_G) + hh
            seg = acs[:, hcol:hcol + 1] - acs_t[hcol:hcol + 1, :]
            m = (cb * jnp.exp(jnp.where(keep, seg, -jnp.inf))).astype(BF16)
            y = y + jnp.where(lane_blk == hh, _dot(m, xdt_b[:, cols]), 0.0)
        y_ref[0, :, cols] = y.astype(y_ref.dtype)


def _ssd_kernel(xf_ref, xfp_ref, xfn_ref, dtf_ref, xb_ref, xbp_ref, xbn_ref, dtb_ref,
                cw_ref, cb_ref, bias_ref, alog_ref, tri_ref, exp_ref, init_ref,
                yf_ref, yb_ref, xs_ref, fin_ref, state_ref):
    s = pl.program_id(1)
    n_chunks = pl.num_programs(1)

    @pl.when(s == 0)
    def _():
        state_ref[...] = init_ref[0]

    common = (cw_ref, cb_ref, bias_ref, alog_ref)
    _ssd_direction(xf_ref, xfp_ref, xfn_ref, dtf_ref, s, n_chunks, *common, tri_ref.at[0], exp_ref.at[0],
                   state_ref.at[0], yf_ref, xs_ref, reverse=False)
    _ssd_direction(xb_ref, xbp_ref, xbn_ref, dtb_ref, n_chunks - 1 - s, n_chunks, *common, tri_ref.at[1],
                   exp_ref.at[1], state_ref.at[1], yb_ref, None, reverse=True)

    @pl.when(s == n_chunks - 1)
    def _():
        fin_ref[0] = state_ref[...]


def _ssd(xbc, dt, conv_w, conv_b, bias, alog, tri, expand, init):
    b, s, _ = xbc.shape
    nc = s // SSD_Q
    per = SSD_Q // V7X_SUBLANES
    last_blk = s // V7X_SUBLANES - 1

    def main_f(bi, i): return (bi, i, 0)
    def prev_f(bi, i): return (bi, jnp.maximum(i * per - 1, 0), 0)
    def next_f(bi, i): return (bi, jnp.minimum((i + 1) * per, last_blk), 0)
    def main_b(bi, i): return (bi, nc - 1 - i, 0)
    def prev_b(bi, i): return (bi, jnp.maximum((nc - 1 - i) * per - 1, 0), 0)
    def next_b(bi, i): return (bi, jnp.minimum((nc - i) * per, last_blk), 0)
    const2 = lambda bi, i: (0, 0)
    const3 = lambda bi, i: (0, 0, 0)
    state_shape = (2, SSD_G, SSD_N, GRP_W)
    return pl.pallas_call(
        _ssd_kernel,
        out_shape=[jax.ShapeDtypeStruct((b, s, SSD_IN), BF16), jax.ShapeDtypeStruct((b, s, SSD_IN), BF16),
                   jax.ShapeDtypeStruct((b, s, SSD_IN), BF16), jax.ShapeDtypeStruct((b,) + state_shape, F32)],
        grid=(b, nc),
        in_specs=[pl.BlockSpec((1, SSD_Q, XBC_W), main_f), pl.BlockSpec((1, V7X_SUBLANES, XBC_W), prev_f),
                  pl.BlockSpec((1, V7X_SUBLANES, XBC_W), next_f), pl.BlockSpec((1, SSD_Q, V7X_LANES), main_f),
                  pl.BlockSpec((1, SSD_Q, XBC_W), main_b), pl.BlockSpec((1, V7X_SUBLANES, XBC_W), prev_b),
                  pl.BlockSpec((1, V7X_SUBLANES, XBC_W), next_b), pl.BlockSpec((1, SSD_Q, V7X_LANES), main_b),
                  pl.BlockSpec((3, XBC_W), const2), pl.BlockSpec((1, XBC_W), const2),
                  pl.BlockSpec((1, V7X_LANES), const2), pl.BlockSpec((1, V7X_LANES), const2),
                  pl.BlockSpec((2, SSD_Q, SSD_Q), const3), pl.BlockSpec((2, V7X_LANES, SSD_IN), const3),
                  pl.BlockSpec((1,) + state_shape, lambda bi, i: (bi, 0, 0, 0, 0))],
        out_specs=[pl.BlockSpec((1, SSD_Q, SSD_IN), main_f), pl.BlockSpec((1, SSD_Q, SSD_IN), main_b),
                   pl.BlockSpec((1, SSD_Q, SSD_IN), main_f),
                   pl.BlockSpec((1,) + state_shape, lambda bi, i: (bi, 0, 0, 0, 0))],
        scratch_shapes=[pltpu.VMEM(state_shape, F32)],
        compiler_params=_params(("parallel", "arbitrary")),
        name="ssd",
    )(xbc, xbc, xbc, dt, xbc, xbc, xbc, dt, conv_w, conv_b, bias, alog, tri, expand, init)


def _out_proj_kernel(x_ref, mod_ref, att_ref, sgu_ref, yf_ref, yb_ref, xs_ref, z_ref,
                     ga_ref, gs_ref, d_ref, gy_ref, w_ref, o_ref):
    att = _rms(att_ref[0].astype(F32), ga_ref[...]).astype(BF16)
    sgu = _rms(sgu_ref[0].astype(F32), gs_ref[...]).astype(BF16)
    y = yf_ref[0].astype(F32) + yb_ref[0].astype(F32) + d_ref[...] * xs_ref[0].astype(F32)
    ssd = _rms(y * _silu(z_ref[0].astype(F32)), gy_ref[...]).astype(BF16)
    mix = (_dot(att, w_ref[:ATT_W, :]) + _dot(sgu, w_ref[ATT_W:ATT_W + SGU_W, :])
           + _dot(ssd, w_ref[ATT_W + SGU_W:, :]))
    o_ref[0] = x_ref[0] + mod_ref[0, 2:3, :] * mix


def _out_proj(x, mod, att, sgu, yf, yb, xs, z, ga, gs, dvec, gy, w_out, tm):
    b, s, d = x.shape
    row = lambda w: pl.BlockSpec((1, tm, w), lambda bi, i: (bi, i, 0))
    vec = lambda w: pl.BlockSpec((1, w), lambda bi, i: (0, 0))
    return pl.pallas_call(
        _out_proj_kernel,
        out_shape=jax.ShapeDtypeStruct((b, s, d), F32),
        grid=(b, s // tm),
        in_specs=[row(d), pl.BlockSpec((1, V7X_SUBLANES, d), lambda bi, i: (bi, 0, 0)),
                  row(ATT_W), row(SGU_W), row(SSD_IN), row(SSD_IN), row(SSD_IN), row(SSD_IN),
                  vec(ATT_W), vec(SGU_W), vec(SSD_IN), vec(SSD_IN),
                  pl.BlockSpec(w_out.shape, lambda bi, i: (0, 0), pipeline_mode=pl.Buffered(1))],
        out_specs=row(d),
        compiler_params=_params(("parallel", "parallel")),
        name="out_proj",
    )(x, mod, att, sgu, yf, yb, xs, z, ga, gs, dvec, gy, w_out)


def _ffn_kernel(x_ref, xp_ref, xn_ref, mod_ref, gn_ref, wg_ref, wu_ref, cw_ref, cb_ref, wd_ref, o_ref,
                h_sc, halo_sc, acc_sc):
    i = pl.program_id(1)
    j = pl.program_id(2)
    shift = mod_ref[0, 3:4, :]
    scale = mod_ref[0, 4:5, :]

    @pl.when(j == 0)
    def _():
        h_sc[...] = (_rms(x_ref[0], gn_ref[...]) * (1.0 + scale) + shift).astype(BF16)
        hp = _rms(xp_ref[0], gn_ref[...]) * (1.0 + scale) + shift
        hn = _rms(xn_ref[0], gn_ref[...]) * (1.0 + scale) + shift
        halo_sc[0:V7X_SUBLANES, :] = jnp.where(i > 0, hp, 0.0).astype(BF16)
        halo_sc[V7X_SUBLANES:, :] = jnp.where(i < pl.num_programs(1) - 1, hn, 0.0).astype(BF16)
        acc_sc[...] = jnp.zeros_like(acc_sc)

    h = h_sc[...]
    tm = h.shape[0]
    gate = _dot(h, wg_ref[...])
    gate_halo = _dot(halo_sc[...], wg_ref[...])
    row = lax.broadcasted_iota(jnp.int32, gate.shape, 0)
    up = jnp.where(row == 0, gate_halo[V7X_SUBLANES - 1:V7X_SUBLANES, :], pltpu.roll(gate, 1, axis=0))
    dn = jnp.where(row == tm - 1, gate_halo[V7X_SUBLANES:V7X_SUBLANES + 1, :], pltpu.roll(gate, tm - 1, axis=0))
    conv = cb_ref[...] + cw_ref[0:1, :] * up + cw_ref[1:2, :] * gate + cw_ref[2:3, :] * dn
    act = (_silu(conv) * _dot(h, wu_ref[...])).astype(BF16)
    acc_sc[...] += _dot(act, wd_ref[...])

    @pl.when(j == pl.num_programs(2) - 1)
    def _():
        o_ref[0] = x_ref[0] + mod_ref[0, 5:6, :] * acc_sc[...]


def _ffn(x, mod, gn, w_gate, w_up, conv_w, conv_b, w_down, tm, tf):
    b, s, d = x.shape
    f = w_gate.shape[1]
    per = tm // V7X_SUBLANES
    last_blk = s // V7X_SUBLANES - 1
    return pl.pallas_call(
        _ffn_kernel,
        out_shape=jax.ShapeDtypeStruct((b, s, d), F32),
        grid=(b, s // tm, f // tf),
        in_specs=[pl.BlockSpec((1, tm, d), lambda bi, i, j: (bi, i, 0)),
                  pl.BlockSpec((1, V7X_SUBLANES, d), lambda bi, i, j: (bi, jnp.maximum(i * per - 1, 0), 0)),
                  pl.BlockSpec((1, V7X_SUBLANES, d), lambda bi, i, j: (bi, jnp.minimum((i + 1) * per, last_blk), 0)),
                  pl.BlockSpec((1, V7X_SUBLANES, d), lambda bi, i, j: (bi, 0, 0)),
                  pl.BlockSpec((1, d), lambda bi, i, j: (0, 0)),
                  pl.BlockSpec((d, tf), lambda bi, i, j: (0, j)),
                  pl.BlockSpec((d, tf), lambda bi, i, j: (0, j)),
                  pl.BlockSpec((3, tf), lambda bi, i, j: (0, j)),
                  pl.BlockSpec((1, tf), lambda bi, i, j: (0, j)),
                  pl.BlockSpec((tf, d), lambda bi, i, j: (j, 0))],
        out_specs=pl.BlockSpec((1, tm, d), lambda bi, i, j: (bi, i, 0)),
        scratch_shapes=[pltpu.VMEM((tm, d), BF16), pltpu.VMEM((2 * V7X_SUBLANES, d), BF16),
                        pltpu.VMEM((tm, d), F32)],
        compiler_params=_params(("parallel", "parallel", "arbitrary")),
        name="ffn",
    )(x, x, x, mod, gn, w_gate, w_up, conv_w, conv_b, w_down)


def _rope_tables(seq):
    rows = seq // GRID_COLS
    r = jnp.repeat(jnp.arange(rows), GRID_COLS).astype(F32)
    c = jnp.tile(jnp.arange(GRID_COLS), rows).astype(F32)
    pairs = ROPE // 4
    freqs = ROPE_THETA ** (-jnp.arange(pairs, dtype=F32) / pairs)
    ar, ac = r[:, None] * freqs, c[:, None] * freqs
    cos = jnp.concatenate([jnp.cos(ar), jnp.cos(ar), jnp.cos(ac), jnp.cos(ac)], axis=-1)
    sin = jnp.concatenate([-jnp.sin(ar), jnp.sin(ar), -jnp.sin(ac), jnp.sin(ac)], axis=-1)
    return jnp.concatenate([cos, sin], axis=-1)


def _pack_w_in(w):
    d = w.shape[0]
    off_ckv, off_kr, off_sgu = Q_LORA, Q_LORA + KV_LORA, Q_LORA + KV_LORA + ROPE
    off_z = off_sgu + 2 * SGU_W
    off_xbc = off_z + SSD_IN
    off_dt = off_xbc + XBC_W
    kr = w[:, off_kr:off_sgu]
    dt = w[:, off_dt:]
    return jnp.concatenate([w[:, :off_kr], kr, kr[:, _ROT_PERM], w[:, off_sgu:off_dt], dt,
                            jnp.zeros((d, V7X_LANES - dt.shape[1]), w.dtype)], axis=1).astype(BF16)


def _pack_w_uq(w):
    w = w.reshape(Q_LORA, N_HEADS, QK)
    rot = w[:, :, NOPE:]
    return jnp.concatenate([w, rot[:, :, _ROT_PERM]], axis=-1).reshape(Q_LORA, N_HEADS * HEAD_PAD).astype(BF16)


def _pad_lanes(v):
    return jnp.pad(v.reshape(1, -1), ((0, 0), (0, V7X_LANES - v.size)))


def kernel(x, c, ctx, c_ctx, w_mod, b_mod, norm_mix, norm_ffn, w_in, q_a_norm, w_uq, kv_a_norm, w_ukv, q_norm, k_norm, attn_out_norm, sgu_norm, sgu_w, sgu_b, gmlp_out_norm, ssd_conv_w, ssd_conv_b, ssd_dt_bias, ssd_a_log, ssd_d, ssd_norm, w_out, ffn_w_gate, ffn_w_up, ffn_conv_w, ffn_conv_b, ffn_w_down):
    batch, seq, d = x.shape
    n_ctx = ctx.shape[1]
    depth = w_mod.shape[0]
    tm = min(512, seq)
    tm_ctx = min(256, n_ctx)
    tf = 512

    cc = jnp.concatenate([c, c_ctx[None], jnp.zeros((V7X_SUBLANES - batch - 1, d), F32)], axis=0)
    mods = _mods(cc, w_mod, b_mod)

    cs_lat = _rope_tables(seq)
    cs_ctx = jnp.concatenate([jnp.ones((n_ctx, ROPE), F32), jnp.zeros((n_ctx, ROPE), F32)], axis=-1)

    ti = np.arange(SSD_Q)
    tri = jnp.asarray(np.stack([ti[:, None] >= ti[None, :], ti[:, None] <= ti[None, :]]), BF16)
    lane_head = np.arange(SSD_IN) // SSD_P
    col = np.arange(V7X_LANES)
    expand = jnp.asarray(np.stack([col[:, None] == lane_head[None, :],
                                   col[:, None] == lane_head[None, :] + SSD_H]), BF16)

    x_lat, x_ctx = x, ctx
    for l in range(depth):
        need_ctx = l < depth - 1
        m = mods[l].reshape(V7X_SUBLANES, 6, d)
        pad = jnp.zeros((V7X_SUBLANES - 6, d), F32)
        mod_lat = jnp.stack([jnp.concatenate([m[bi], pad]) for bi in range(batch)])
        mod_ctx = jnp.stack([jnp.concatenate([m[batch], pad])] * batch)

        w_in_p = _pack_w_in(w_in[l])
        w_uq_p = _pack_w_uq(w_uq[l])
        w_ukv_b = w_ukv[l].astype(BF16)
        gq, gk = q_norm[l], k_norm[l]
        gq1, gk1 = gq[None, :NOPE], gk[None, :NOPE]
        gq2 = jnp.concatenate([gq[NOPE:], gq[NOPE:][_ROT_PERM]])[None]
        gk2 = jnp.concatenate([gk[NOPE:], gk[NOPE:][_ROT_PERM]])[None]
        an_q, an_kv = q_a_norm[l][None], kv_a_norm[l][None]
        sgu_w_b = sgu_w[l].astype(BF16)
        sgu_b_b = jnp.broadcast_to(sgu_b[l][:, :, None], (SGU_G, SGU_CH, SGU_CH))
        conv_w, conv_b = ssd_conv_w[l], ssd_conv_b[l][None]
        bias, alog = _pad_lanes(ssd_dt_bias[l]), _pad_lanes(ssd_a_log[l])
        dvec = jnp.repeat(ssd_d[l], SSD_P)[None]
        w_out_b = w_out[l].astype(BF16)
        wg, wu, wd = ffn_w_gate[l].astype(BF16), ffn_w_up[l].astype(BF16), ffn_w_down[l].astype(BF16)
        fcw, fcb = ffn_conv_w[l], ffn_conv_b[l][None]

        def mixers(xs_in, mod, cs, t_rows, init_state, k_c=None, v_c=None, outputs=True):
            pq, pkv, psgu, pz, pxbc, pdt = _in_proj(xs_in, mod, norm_mix[l][None], w_in_p, t_rows)
            k, v = _kv_proj(pkv, an_kv, w_ukv_b, gk1, gk2, cs, t_rows)
            yf, yb, xs, fin = _ssd(pxbc, pdt, conv_w, conv_b, bias, alog, tri, expand, init_state)
            if not outputs:
                return None, k, v, fin
            q = _q_proj(pq, an_q, w_uq_p, gq1, gq2, cs, t_rows)
            if k_c is None:
                att = _attention(q, k, v, tq=t_rows)
            else:
                att = _attention(q, k_c, v_c, k, v, tq=t_rows)
            sgu = _sgu(psgu, sgu_norm[l][None], sgu_w_b, sgu_b_b, t_rows)
            out = _out_proj(xs_in, mod, att, sgu, yf, yb, xs, pz, attn_out_norm[l][None],
                            gmlp_out_norm[l][None], dvec, ssd_norm[l][None], w_out_b, t_rows)
            return out, k, v, fin

        zero_state = jnp.zeros((batch, 2, SSD_G, SSD_N, GRP_W), F32)
        x_ctx_mid, k_c, v_c, ctx_state = mixers(x_ctx, mod_ctx, cs_ctx, tm_ctx, zero_state, outputs=need_ctx)
        x_lat, _, _, _ = mixers(x_lat, mod_lat, cs_lat, tm, ctx_state, k_c, v_c)
        x_lat = _ffn(x_lat, mod_lat, norm_ffn[l][None], wg, wu, fcw, fcb, wd, tm, tf)
        if need_ctx:
            x_ctx = _ffn(x_ctx_mid, mod_ctx, norm_ffn[l][None], wg, wu, fcw, fcb, wd, tm_ctx, tf)
    return x_lat
```

```python
import functools
import math

import jax
import jax.numpy as jnp
import numpy as np
from jax import lax
from jax.experimental import pallas as pl
from jax.experimental.pallas import tpu as pltpu

F32 = jnp.float32
BF16 = jnp.bfloat16
NORM_EPS = 1e-6

GRID_COLS = 64
N_HEADS = 8
NOPE = 128
ROPE = 64
QK = NOPE + ROPE
V_DIM = 128
Q_LORA = 768
KV_LORA = 512
ROPE_THETA = 10000.0
SM_SCALE = 1.0 / math.sqrt(QK)
LOG2_E = math.log2(math.e)
HEAD_PAD = 256
ATT_W = N_HEADS * V_DIM
SGU_G = 4
SGU_CH = 128
SGU_W = 512
SSD_H = 8
SSD_P = 64
SSD_IN = SSD_H * SSD_P
SSD_N = 128
SSD_G = 2
SSD_Q = 128
XBC_W = SSD_IN + 2 * SSD_G * SSD_N
GRP_W = (SSD_H // SSD_G) * SSD_P

V7X_LANES = 128
V7X_SUBLANES = 8
V7X_VMEM_BYTES = 64 * 1024 * 1024
VMEM_LIMIT = 56 * 1024 * 1024

SEG_Q = (0, Q_LORA)
SEG_KV = (SEG_Q[1], SEG_Q[1] + KV_LORA + 2 * V7X_LANES)
SEG_SGU = (SEG_KV[1], SEG_KV[1] + 2 * SGU_W)
SEG_Z = (SEG_SGU[1], SEG_SGU[1] + SSD_IN)
SEG_XBC = (SEG_Z[1], SEG_Z[1] + XBC_W)
SEG_DT = (SEG_XBC[1], SEG_XBC[1] + V7X_LANES)
P_PAD = SEG_DT[1]
Q_HEAD_COLS = NOPE + 2 * V7X_LANES

_ROT_PERM = np.concatenate([np.arange(16, 32), np.arange(0, 16), np.arange(48, 64), np.arange(32, 48)])


def _params(sem, vmem=VMEM_LIMIT):
    return pltpu.CompilerParams(dimension_semantics=sem, vmem_limit_bytes=vmem)


def _dot(a, b):
    return jnp.dot(a, b, preferred_element_type=F32)


def _dot_nt(a, b):
    return lax.dot_general(a, b, (((1,), (1,)), ((), ())), preferred_element_type=F32)


def _rms(t, gain):
    return t * lax.rsqrt(jnp.mean(t * t, axis=-1, keepdims=True) + NORM_EPS) * gain


def _silu(t):
    return t * (1.0 / (1.0 + jnp.exp(-t)))


def _gelu_tanh(t):
    return 0.5 * t * (1.0 + jnp.tanh(math.sqrt(2.0 / math.pi) * (t + 0.044715 * (t * t * t))))


def _softplus(t):
    return jnp.maximum(t, 0.0) + jnp.log1p(jnp.exp(-jnp.abs(t)))


def _split3(t):
    hi = t.astype(BF16)
    r1 = t - hi.astype(F32)
    mid = r1.astype(BF16)
    lo = (r1 - mid.astype(F32)).astype(BF16)
    return hi, mid, lo


def _mods_kernel(c_ref, w_ref, b_ref, o_ref):
    s = _silu(c_ref[...])
    s_hi = s.astype(BF16)
    s_lo = (s - s_hi.astype(F32)).astype(BF16)
    w = w_ref[0]
    w_hi = w.astype(BF16)
    w_lo = (w - w_hi.astype(F32)).astype(BF16)
    o_ref[0] = _dot(s_hi, w_hi) + _dot(s_lo, w_hi) + _dot(s_hi, w_lo) + b_ref[0]


def _mods(cc, w_mod, b_mod):
    depth, d, n6 = w_mod.shape
    tn = 1024
    return pl.pallas_call(
        _mods_kernel,
        out_shape=jax.ShapeDtypeStruct((depth, V7X_SUBLANES, n6), F32),
        grid=(depth, n6 // tn),
        in_specs=[pl.BlockSpec((V7X_SUBLANES, d), lambda l, j: (0, 0)),
                  pl.BlockSpec((1, d, tn), lambda l, j: (l, 0, j)),
                  pl.BlockSpec((1, 1, tn), lambda l, j: (l, 0, j))],
        out_specs=pl.BlockSpec((1, V7X_SUBLANES, tn), lambda l, j: (l, 0, j)),
        compiler_params=_params(("parallel", "parallel")),
        name="mods",
    )(cc, w_mod, b_mod.reshape(depth, 1, n6))


def _in_proj_kernel(x_ref, mod_ref, g_ref, w_ref, q_ref, kv_ref, sgu_ref, z_ref, xbc_ref, dt_ref):
    x = x_ref[0]
    shift = mod_ref[0, 0:1, :]
    scale = mod_ref[0, 1:2, :]
    h = (_rms(x, g_ref[...]) * (1.0 + scale) + shift).astype(BF16)
    for ref, (a, b) in ((q_ref, SEG_Q), (kv_ref, SEG_KV), (sgu_ref, SEG_SGU), (z_ref, SEG_Z),
                        (xbc_ref, SEG_XBC), (dt_ref, SEG_DT)):
        ref[0] = _dot(h, w_ref[:, a:b]).astype(ref.dtype)


def _in_proj(x, mod, gain, w_in_p, tm):
    b, s, d = x.shape
    segs = (SEG_Q, SEG_KV, SEG_SGU, SEG_Z, SEG_XBC, SEG_DT)
    dts = (BF16, BF16, BF16, BF16, BF16, F32)
    return pl.pallas_call(
        _in_proj_kernel,
        out_shape=[jax.ShapeDtypeStruct((b, s, hi - lo), dt) for (lo, hi), dt in zip(segs, dts)],
        grid=(b, s // tm),
        in_specs=[pl.BlockSpec((1, tm, d), lambda bi, i: (bi, i, 0)),
                  pl.BlockSpec((1, V7X_SUBLANES, d), lambda bi, i: (bi, 0, 0)),
                  pl.BlockSpec((1, d), lambda bi, i: (0, 0)),
                  pl.BlockSpec((d, P_PAD), lambda bi, i: (0, 0), pipeline_mode=pl.Buffered(1))],
        out_specs=[pl.BlockSpec((1, tm, hi - lo), lambda bi, i: (bi, i, 0)) for lo, hi in segs],
        compiler_params=_params(("parallel", "parallel")),
        name="in_proj",
    )(x, mod, gain, w_in_p)


def _q_proj_kernel(cq_ref, an_ref, w_ref, g_ref, g2_ref, cs_ref, q_ref):
    cq = cq_ref[0].astype(F32)
    cqn = _rms(cq, an_ref[...]).astype(BF16)
    gcs = g2_ref[...] * cs_ref[...]
    gc, gs = gcs[:, :V7X_LANES], gcs[:, V7X_LANES:]
    for h in range(N_HEADS):
        y = _dot(cqn, w_ref[:, h * Q_HEAD_COLS:(h + 1) * Q_HEAD_COLS])
        a, r, rp = y[:, :NOPE], y[:, NOPE:2 * NOPE], y[:, 2 * NOPE:]
        ssq = jnp.sum(a * a + r * r, axis=-1, keepdims=True)
        rs = lax.rsqrt(ssq * (1.0 / QK) + NORM_EPS) * (SM_SCALE * LOG2_E)
        q_ref[0, :, h * HEAD_PAD:h * HEAD_PAD + NOPE] = (a * g_ref[...] * rs).astype(BF16)
        q_ref[0, :, h * HEAD_PAD + NOPE:(h + 1) * HEAD_PAD] = ((r * gc + rp * gs) * rs).astype(BF16)


def _q_proj(cq, an, w_uq_p, g, g2, cs, tm):
    b, s, _ = cq.shape
    width = N_HEADS * HEAD_PAD
    return pl.pallas_call(
        _q_proj_kernel,
        out_shape=jax.ShapeDtypeStruct((b, s, width), BF16),
        grid=(b, s // tm),
        in_specs=[pl.BlockSpec((1, tm, Q_LORA), lambda bi, i: (bi, i, 0)),
                  pl.BlockSpec((1, Q_LORA), lambda bi, i: (0, 0)),
                  pl.BlockSpec((Q_LORA, N_HEADS * Q_HEAD_COLS), lambda bi, i: (0, 0)),
                  pl.BlockSpec((1, NOPE), lambda bi, i: (0, 0)),
                  pl.BlockSpec((1, 2 * V7X_LANES), lambda bi, i: (0, 0)),
                  pl.BlockSpec((tm, 2 * V7X_LANES), lambda bi, i: (i, 0))],
        out_specs=pl.BlockSpec((1, tm, width), lambda bi, i: (bi, i, 0)),
        compiler_params=_params(("parallel", "parallel")),
        name="q_proj",
    )(cq, an, w_uq_p, g, g2, cs)


def _kv_proj_kernel(t_ref, an_ref, w_ref, g_ref, g2_ref, cs_ref, k_ref, v_ref):
    ckv = t_ref[0, :, :KV_LORA].astype(F32)
    kr = t_ref[0, :, KV_LORA:KV_LORA + V7X_LANES].astype(F32)
    krp = t_ref[0, :, KV_LORA + V7X_LANES:].astype(F32)
    ckvn = _rms(ckv, an_ref[...]).astype(BF16)
    gcs = g2_ref[...] * cs_ref[...]
    rot = kr * gcs[:, :V7X_LANES] + krp * gcs[:, V7X_LANES:]
    kr_sq = kr * kr
    lane = lax.broadcasted_iota(jnp.int32, kr.shape, 1)
    ones_col = jnp.where(lane == 0, 1.0, 0.0).astype(BF16)
    for h in range(N_HEADS):
        y = _dot(ckvn, w_ref[:, h * 2 * NOPE:(h + 1) * 2 * NOPE])
        kn = y[:, :NOPE]
        rs = lax.rsqrt(jnp.sum(kn * kn + kr_sq, axis=-1, keepdims=True) * (1.0 / QK) + NORM_EPS)
        k_ref[0, :, h * HEAD_PAD:h * HEAD_PAD + NOPE] = (kn * g_ref[...] * rs).astype(BF16)
        k_ref[0, :, h * HEAD_PAD + NOPE:(h + 1) * HEAD_PAD] = (rot * rs).astype(BF16)
        v_ref[0, :, h * HEAD_PAD:h * HEAD_PAD + V_DIM] = y[:, NOPE:].astype(BF16)
        v_ref[0, :, h * HEAD_PAD + V_DIM:(h + 1) * HEAD_PAD] = ones_col


def _kv_proj(t, an, w_ukv, g, g2, cs, tm):
    b, s, wt = t.shape
    return pl.pallas_call(
        _kv_proj_kernel,
        out_shape=[jax.ShapeDtypeStruct((b, s, N_HEADS * HEAD_PAD), BF16),
                   jax.ShapeDtypeStruct((b, s, N_HEADS * HEAD_PAD), BF16)],
        grid=(b, s // tm),
        in_specs=[pl.BlockSpec((1, tm, wt), lambda bi, i: (bi, i, 0)),
                  pl.BlockSpec((1, KV_LORA), lambda bi, i: (0, 0)),
                  pl.BlockSpec((KV_LORA, N_HEADS * 2 * NOPE), lambda bi, i: (0, 0)),
                  pl.BlockSpec((1, NOPE), lambda bi, i: (0, 0)),
                  pl.BlockSpec((1, 2 * V7X_LANES), lambda bi, i: (0, 0)),
                  pl.BlockSpec((tm, 2 * V7X_LANES), lambda bi, i: (i, 0))],
        out_specs=[pl.BlockSpec((1, tm, N_HEADS * HEAD_PAD), lambda bi, i: (bi, i, 0)),
                   pl.BlockSpec((1, tm, N_HEADS * HEAD_PAD), lambda bi, i: (bi, i, 0))],
        compiler_params=_params(("parallel", "parallel")),
        name="kv_proj",
    )(t, an, w_ukv, g, g2, cs)


def _attn_kernel(*refs, n_lat, tk):
    if n_lat:
        q_ref, kc_ref, vc_ref, kl_ref, vl_ref, o_ref, s0_ref, s1_ref = refs
    else:
        q_ref, kc_ref, vc_ref, o_ref = refs
    q = q_ref[0]
    tq = q.shape[0]

    def update(carry, s, v):
        m, acc = carry
        m_new = jnp.maximum(m, jnp.max(s, axis=-1, keepdims=True))
        p = jnp.exp2(s - m_new).astype(BF16)
        return m_new, jnp.exp2(m - m_new) * acc + _dot(p, v)

    carry = (jnp.full((tq, 1), -jnp.inf, F32), jnp.zeros((tq, HEAD_PAD), F32))
    if n_lat:
        n_chunks = n_lat // tk

        slots = (s0_ref, s1_ref)

        def scores(j):
            slots[j % 2][...] = _dot_nt(q, kl_ref[0, j * tk:(j + 1) * tk, :])

        scores(0)
        carry = update(carry, _dot_nt(q, kc_ref[0]), vc_ref[0])
        for j in range(n_chunks):
            if j + 1 < n_chunks:
                scores(j + 1)
            carry = update(carry, slots[j % 2][...], vl_ref[0, j * tk:(j + 1) * tk, :])
    else:
        carry = update(carry, _dot_nt(q, kc_ref[0]), vc_ref[0])
    _, acc = carry
    o_ref[0] = (acc[:, :V_DIM] * (1.0 / acc[:, V_DIM:V_DIM + 1])).astype(o_ref.dtype)


def _attention(q, k_c, v_c, k_l=None, v_l=None, *, tq, tk):
    b, s, _ = q.shape
    n_ctx = k_c.shape[1]
    n_lat = 0 if k_l is None else k_l.shape[1]
    in_specs = [pl.BlockSpec((1, tq, HEAD_PAD), lambda bi, h, i: (bi, i, h)),
                pl.BlockSpec((1, n_ctx, HEAD_PAD), lambda bi, h, i: (bi, 0, h)),
                pl.BlockSpec((1, n_ctx, HEAD_PAD), lambda bi, h, i: (bi, 0, h))]
    args = [q, k_c, v_c]
    scratch = []
    if n_lat:
        tk = min(tk, n_lat)
        assert n_lat % tk == 0
        in_specs += [pl.BlockSpec((1, n_lat, HEAD_PAD), lambda bi, h, i: (bi, 0, h)),
                     pl.BlockSpec((1, n_lat, HEAD_PAD), lambda bi, h, i: (bi, 0, h))]
        args += [k_l, v_l]
        scratch = [pltpu.VMEM((tq, tk), F32), pltpu.VMEM((tq, tk), F32)]
    return pl.pallas_call(
        functools.partial(_attn_kernel, n_lat=n_lat, tk=tk),
        out_shape=jax.ShapeDtypeStruct((b, s, ATT_W), BF16),
        grid=(b, N_HEADS, s // tq),
        in_specs=in_specs,
        out_specs=pl.BlockSpec((1, tq, V_DIM), lambda bi, h, i: (bi, i, h)),
        scratch_shapes=scratch,
        compiler_params=_params(("parallel", "parallel", "arbitrary")),
        name="attention",
    )(*args)


def _sgu_kernel(p_ref, gn_ref, w_ref, b_ref, o_ref, *, n_chunks):
    for c in range(n_chunks):
        rows = slice(c * SGU_CH, (c + 1) * SGU_CH)
        z = _gelu_tanh(p_ref[0, rows, :].astype(F32))
        u = z[:, :SGU_W]
        vn = _rms(z[:, SGU_W:], gn_ref[...]).astype(BF16)
        for g in range(SGU_G):
            cols = slice(g * SGU_CH, (g + 1) * SGU_CH)
            mixed = _dot(w_ref[g], vn[:, cols]) + b_ref[g]
            o_ref[0, rows, cols] = (u[:, cols] * mixed).astype(o_ref.dtype)


def _sgu(p, gn, w_s, b_b, rows):
    b, s, _ = p.shape
    return pl.pallas_call(
        functools.partial(_sgu_kernel, n_chunks=rows // SGU_CH),
        out_shape=jax.ShapeDtypeStruct((b, s, SGU_W), BF16),
        grid=(b, s // rows),
        in_specs=[pl.BlockSpec((1, rows, 2 * SGU_W), lambda bi, i: (bi, i, 0)),
                  pl.BlockSpec((1, SGU_W), lambda bi, i: (0, 0)),
                  pl.BlockSpec((SGU_G, SGU_CH, SGU_CH), lambda bi, i: (0, 0, 0)),
                  pl.BlockSpec((SGU_G, SGU_CH, SGU_CH), lambda bi, i: (0, 0, 0))],
        out_specs=pl.BlockSpec((1, rows, SGU_W), lambda bi, i: (bi, i, 0)),
        compiler_params=_params(("parallel", "parallel")),
        name="sgu",
    )(p, gn, w_s, b_b)


def _ssd_direction(xm_ref, xp_ref, xn_ref, dt_ref, chunk, n_chunks, cw_ref, cb_ref, bias_ref, alog_ref,
                   tri_ref, exp_ref, state_ref, y_ref, xs_ref, *, reverse):
    q = SSD_Q
    x = xm_ref[0].astype(F32)
    row = lax.broadcasted_iota(jnp.int32, x.shape, 0)
    prev_row = jnp.where(chunk > 0, xp_ref[0, V7X_SUBLANES - 1:V7X_SUBLANES, :].astype(F32), 0.0)
    next_row = jnp.where(chunk < n_chunks - 1, xn_ref[0, 0:1, :].astype(F32), 0.0)
    up = jnp.where(row == 0, prev_row, pltpu.roll(x, 1, axis=0))
    dn = jnp.where(row == q - 1, next_row, pltpu.roll(x, q - 1, axis=0))
    xbc = _silu(cb_ref[...] + cw_ref[0:1, :] * up + cw_ref[1:2, :] * x + cw_ref[2:3, :] * dn)
    xs = xbc[:, :SSD_IN]
    if xs_ref is not None:
        xs_ref[0] = xs.astype(xs_ref.dtype)

    dt = _softplus(dt_ref[0] + bias_ref[...])
    a_dt = dt * (-jnp.exp(alog_ref[...]))
    tri = tri_ref[...]
    acs = sum(_dot(tri, part) for part in _split3(a_dt))
    acs_t = acs.T
    expand = exp_ref[...]
    acs_x = sum(_dot(part, expand) for part in _split3(acs))
    dt_hi = dt.astype(BF16)
    dt_x = _dot(dt_hi, expand) + _dot((dt - dt_hi.astype(F32)).astype(BF16), expand)
    last = acs_x[0:1, :] if reverse else acs_x[q - 1:q, :]
    xdt = xs * dt_x
    xdt_b = xdt.astype(BF16)
    x_end = (xdt * jnp.exp(last - acs_x)).astype(BF16)
    decay_out = jnp.exp(acs_x)
    chunk_decay = jnp.exp(last)

    ti = lax.broadcasted_iota(jnp.int32, (q, q), 0)
    tj = lax.broadcasted_iota(jnp.int32, (q, q), 1)
    keep = (ti <= tj) if reverse else (ti >= tj)
    lane_blk = lax.shift_right_logical(lax.broadcasted_iota(jnp.int32, (q, GRP_W), 1), int(math.log2(SSD_P)))
    head0 = SSD_H if reverse else 0
    for g in range(SSD_G):
        bm = xbc[:, SSD_IN + g * SSD_N:SSD_IN + (g + 1) * SSD_N]
        cm = xbc[:, SSD_IN + (SSD_G + g) * SSD_N:SSD_IN + (SSD_G + g + 1) * SSD_N].astype(BF16)
        cb = _dot_nt(cm, bm.astype(BF16))
        cols = slice(g * GRP_W, (g + 1) * GRP_W)
        state = state_ref[g]
        y = _dot(cm, state.astype(BF16)) * decay_out[:, cols]
        state_ref[g] = state * chunk_decay[:, cols] + _dot(bm.T.astype(BF16), x_end[:, cols])
        for hh in range(SSD_H // SSD_G):
            hcol = head0 + g * (SSD_H // SSD_G) + hh
            seg = acs[:, hcol:hcol + 1] - acs_t[hcol:hcol + 1, :]
            m = (cb * jnp.exp(jnp.where(keep, seg, -jnp.inf))).astype(BF16)
            y = y + jnp.where(lane_blk == hh, _dot(m, xdt_b[:, cols]), 0.0)
        y_ref[0, :, cols] = y.astype(y_ref.dtype)


def _ssd_kernel(xf_ref, xfp_ref, xfn_ref, dtf_ref, xb_ref, xbp_ref, xbn_ref, dtb_ref,
                cw_ref, cb_ref, bias_ref, alog_ref, tri_ref, exp_ref, init_ref,
                yf_ref, yb_ref, xs_ref, fin_ref, state_ref):
    s = pl.program_id(1)
    n_chunks = pl.num_programs(1)

    @pl.when(s == 0)
    def _():
        state_ref[...] = init_ref[0]

    common = (cw_ref, cb_ref, bias_ref, alog_ref)
    _ssd_direction(xf_ref, xfp_ref, xfn_ref, dtf_ref, s, n_chunks, *common, tri_ref.at[0], exp_ref.at[0],
                   state_ref.at[0], yf_ref, xs_ref, reverse=False)
    _ssd_direction(xb_ref, xbp_ref, xbn_ref, dtb_ref, n_chunks - 1 - s, n_chunks, *common, tri_ref.at[1],
                   exp_ref.at[1], state_ref.at[1], yb_ref, None, reverse=True)

    @pl.when(s == n_chunks - 1)
    def _():
        fin_ref[0] = state_ref[...]


def _ssd(xbc, dt, conv_w, conv_b, bias, alog, tri, expand, init):
    b, s, _ = xbc.shape
    nc = s // SSD_Q
    per = SSD_Q // V7X_SUBLANES
    last_blk = s // V7X_SUBLANES - 1

    def main_f(bi, i): return (bi, i, 0)
    def prev_f(bi, i): return (bi, jnp.maximum(i * per - 1, 0), 0)
    def next_f(bi, i): return (bi, jnp.minimum((i + 1) * per, last_blk), 0)
    def main_b(bi, i): return (bi, nc - 1 - i, 0)
    def prev_b(bi, i): return (bi, jnp.maximum((nc - 1 - i) * per - 1, 0), 0)
    def next_b(bi, i): return (bi, jnp.minimum((nc - i) * per, last_blk), 0)
    const2 = lambda bi, i: (0, 0)
    const3 = lambda bi, i: (0, 0, 0)
    state_shape = (2, SSD_G, SSD_N, GRP_W)
    return pl.pallas_call(
        _ssd_kernel,
        out_shape=[jax.ShapeDtypeStruct((b, s, SSD_IN), BF16), jax.ShapeDtypeStruct((b, s, SSD_IN), BF16),
                   jax.ShapeDtypeStruct((b, s, SSD_IN), BF16), jax.ShapeDtypeStruct((b,) + state_shape, F32)],
        grid=(b, nc),
        in_specs=[pl.BlockSpec((1, SSD_Q, XBC_W), main_f), pl.BlockSpec((1, V7X_SUBLANES, XBC_W), prev_f),
                  pl.BlockSpec((1, V7X_SUBLANES, XBC_W), next_f), pl.BlockSpec((1, SSD_Q, V7X_LANES), main_f),
                  pl.BlockSpec((1, SSD_Q, XBC_W), main_b), pl.BlockSpec((1, V7X_SUBLANES, XBC_W), prev_b),
                  pl.BlockSpec((1, V7X_SUBLANES, XBC_W), next_b), pl.BlockSpec((1, SSD_Q, V7X_LANES), main_b),
                  pl.BlockSpec((3, XBC_W), const2), pl.BlockSpec((1, XBC_W), const2),
                  pl.BlockSpec((1, V7X_LANES), const2), pl.BlockSpec((1, V7X_LANES), const2),
                  pl.BlockSpec((2, SSD_Q, SSD_Q), const3), pl.BlockSpec((2, V7X_LANES, SSD_IN), const3),
                  pl.BlockSpec((1,) + state_shape, lambda bi, i: (bi, 0, 0, 0, 0))],
        out_specs=[pl.BlockSpec((1, SSD_Q, SSD_IN), main_f), pl.BlockSpec((1, SSD_Q, SSD_IN), main_b),
                   pl.BlockSpec((1, SSD_Q, SSD_IN), main_f),
                   pl.BlockSpec((1,) + state_shape, lambda bi, i: (bi, 0, 0, 0, 0))],
        scratch_shapes=[pltpu.VMEM(state_shape, F32)],
        compiler_params=_params(("parallel", "arbitrary")),
        name="ssd",
    )(xbc, xbc, xbc, dt, xbc, xbc, xbc, dt, conv_w, conv_b, bias, alog, tri, expand, init)


def _out_proj_kernel(x_ref, mod_ref, att_ref, sgu_ref, yf_ref, yb_ref, xs_ref, z_ref,
                     ga_ref, gs_ref, d_ref, gy_ref, w_ref, o_ref):
    att = _rms(att_ref[0].astype(F32), ga_ref[...]).astype(BF16)
    sgu = _rms(sgu_ref[0].astype(F32), gs_ref[...]).astype(BF16)
    y = yf_ref[0].astype(F32) + yb_ref[0].astype(F32) + d_ref[...] * xs_ref[0].astype(F32)
    ssd = _rms(y * _silu(z_ref[0].astype(F32)), gy_ref[...]).astype(BF16)
    mix = (_dot(att, w_ref[:ATT_W, :]) + _dot(sgu, w_ref[ATT_W:ATT_W + SGU_W, :])
           + _dot(ssd, w_ref[ATT_W + SGU_W:, :]))
    o_ref[0] = x_ref[0] + mod_ref[0, 2:3, :] * mix


def _out_proj(x, mod, att, sgu, yf, yb, xs, z, ga, gs, dvec, gy, w_out, tm):
    b, s, d = x.shape
    row = lambda w: pl.BlockSpec((1, tm, w), lambda bi, i: (bi, i, 0))
    vec = lambda w: pl.BlockSpec((1, w), lambda bi, i: (0, 0))
    return pl.pallas_call(
        _out_proj_kernel,
        out_shape=jax.ShapeDtypeStruct((b, s, d), F32),
        grid=(b, s // tm),
        in_specs=[row(d), pl.BlockSpec((1, V7X_SUBLANES, d), lambda bi, i: (bi, 0, 0)),
                  row(ATT_W), row(SGU_W), row(SSD_IN), row(SSD_IN), row(SSD_IN), row(SSD_IN),
                  vec(ATT_W), vec(SGU_W), vec(SSD_IN), vec(SSD_IN),
                  pl.BlockSpec(w_out.shape, lambda bi, i: (0, 0), pipeline_mode=pl.Buffered(1))],
        out_specs=row(d),
        compiler_params=_params(("parallel", "parallel")),
        name="out_proj",
    )(x, mod, att, sgu, yf, yb, xs, z, ga, gs, dvec, gy, w_out)


def _ffn_kernel(x_ref, xp_ref, xn_ref, mod_ref, gn_ref, wg_ref, wu_ref, cw_ref, cb_ref, wd_ref, o_ref,
                h_sc, halo_sc, act_sc, acc_sc):
    i = pl.program_id(1)
    j = pl.program_id(2)
    n_f = pl.num_programs(2) - 1
    shift = mod_ref[0, 3:4, :]
    scale = mod_ref[0, 4:5, :]

    def activation(slot):
        h = h_sc[...]
        tm = h.shape[0]
        gate = _dot(h, wg_ref[...])
        gate_halo = _dot(halo_sc[...], wg_ref[...])
        row = lax.broadcasted_iota(jnp.int32, gate.shape, 0)
        up = jnp.where(row == 0, gate_halo[V7X_SUBLANES - 1:V7X_SUBLANES, :], pltpu.roll(gate, 1, axis=0))
        dn = jnp.where(row == tm - 1, gate_halo[V7X_SUBLANES:V7X_SUBLANES + 1, :],
                       pltpu.roll(gate, tm - 1, axis=0))
        conv = cb_ref[...] + cw_ref[0:1, :] * up + cw_ref[1:2, :] * gate + cw_ref[2:3, :] * dn
        act_sc[slot] = (_silu(conv) * _dot(h, wu_ref[...])).astype(BF16)

    def down(slot):
        return _dot(act_sc[slot], wd_ref[...])

    @pl.when(j == 0)
    def _():
        h_sc[...] = (_rms(x_ref[0], gn_ref[...]) * (1.0 + scale) + shift).astype(BF16)
        hp = _rms(xp_ref[0], gn_ref[...]) * (1.0 + scale) + shift
        hn = _rms(xn_ref[0], gn_ref[...]) * (1.0 + scale) + shift
        halo_sc[0:V7X_SUBLANES, :] = jnp.where(i > 0, hp, 0.0).astype(BF16)
        halo_sc[V7X_SUBLANES:, :] = jnp.where(i < pl.num_programs(1) - 1, hn, 0.0).astype(BF16)
        activation(0)

    @pl.when(j == 1)
    def _():
        acc_sc[...] = down(0)
        activation(1)

    @pl.when((j > 1) & (j < n_f))
    def _():
        acc_sc[...] += down(lax.rem(j - 1, 2))
        activation(lax.rem(j, 2))

    @pl.when(j == n_f)
    def _():
        o_ref[0] = x_ref[0] + mod_ref[0, 5:6, :] * (acc_sc[...] + down(lax.rem(j - 1, 2)))


def _ffn(x, mod, gn, w_gate, w_up, conv_w, conv_b, w_down, tm, tf):
    b, s, d = x.shape
    f = w_gate.shape[1]
    per = tm // V7X_SUBLANES
    last_blk = s // V7X_SUBLANES - 1
    n_f = f // tf
    assert f % tf == 0 and n_f >= 2
    cur = lambda bi, i, j: (0, jnp.minimum(j, n_f - 1))
    return pl.pallas_call(
        _ffn_kernel,
        out_shape=jax.ShapeDtypeStruct((b, s, d), F32),
        grid=(b, s // tm, n_f + 1),
        in_specs=[pl.BlockSpec((1, tm, d), lambda bi, i, j: (bi, i, 0)),
                  pl.BlockSpec((1, V7X_SUBLANES, d), lambda bi, i, j: (bi, jnp.maximum(i * per - 1, 0), 0)),
                  pl.BlockSpec((1, V7X_SUBLANES, d), lambda bi, i, j: (bi, jnp.minimum((i + 1) * per, last_blk), 0)),
                  pl.BlockSpec((1, V7X_SUBLANES, d), lambda bi, i, j: (bi, 0, 0)),
                  pl.BlockSpec((1, d), lambda bi, i, j: (0, 0)),
                  pl.BlockSpec((d, tf), cur),
                  pl.BlockSpec((d, tf), cur),
                  pl.BlockSpec((3, tf), cur),
                  pl.BlockSpec((1, tf), cur),
                  pl.BlockSpec((tf, d), lambda bi, i, j: (jnp.maximum(j - 1, 0), 0))],
        out_specs=pl.BlockSpec((1, tm, d), lambda bi, i, j: (bi, i, 0)),
        scratch_shapes=[pltpu.VMEM((tm, d), BF16), pltpu.VMEM((2 * V7X_SUBLANES, d), BF16),
                        pltpu.VMEM((2, tm, tf), BF16), pltpu.VMEM((tm, d), F32)],
        compiler_params=_params(("parallel", "parallel", "arbitrary")),
        name="ffn",
    )(x, x, x, mod, gn, w_gate, w_up, conv_w, conv_b, w_down)


def _rope_tables(seq):
    rows = seq // GRID_COLS
    pairs = ROPE // 4
    freqs = ROPE_THETA ** (-jnp.arange(pairs, dtype=F32) / pairs)
    ar = jnp.arange(rows, dtype=F32)[:, None] * freqs
    ac = jnp.arange(GRID_COLS, dtype=F32)[:, None] * freqs
    by_row = lambda t: jnp.repeat(t, GRID_COLS, axis=0)
    by_col = lambda t: jnp.tile(t, (rows, 1))
    cr, sr, cc, sc = by_row(jnp.cos(ar)), by_row(jnp.sin(ar)), by_col(jnp.cos(ac)), by_col(jnp.sin(ac))
    cos = jnp.concatenate([cr, cr, cc, cc], axis=-1)
    sin = jnp.concatenate([-sr, sr, -sc, sc], axis=-1)
    zeros = jnp.zeros_like(cos)
    return jnp.concatenate([cos, zeros, sin, zeros], axis=-1)


def _pack_w_in(w):
    d = w.shape[0]
    off_ckv, off_kr, off_sgu = Q_LORA, Q_LORA + KV_LORA, Q_LORA + KV_LORA + ROPE
    off_z = off_sgu + 2 * SGU_W
    off_xbc = off_z + SSD_IN
    off_dt = off_xbc + XBC_W
    kr = w[:, off_kr:off_sgu]
    dt = w[:, off_dt:]
    z64 = jnp.zeros((d, V7X_LANES - ROPE), w.dtype)
    return jnp.concatenate([w[:, :off_kr], kr, z64, kr[:, _ROT_PERM], z64, w[:, off_sgu:off_dt], dt,
                            jnp.zeros((d, V7X_LANES - dt.shape[1]), w.dtype)], axis=1).astype(BF16)


def _pack_w_uq(w):
    w = w.reshape(Q_LORA, N_HEADS, QK)
    rot = w[:, :, NOPE:]
    z64 = jnp.zeros((Q_LORA, N_HEADS, V7X_LANES - ROPE), w.dtype)
    return jnp.concatenate([w, z64, rot[:, :, _ROT_PERM], z64], axis=-1).reshape(
        Q_LORA, N_HEADS * Q_HEAD_COLS).astype(BF16)


def _rot_gains(g):
    z64 = jnp.zeros((V7X_LANES - ROPE,), g.dtype)
    return jnp.concatenate([g[NOPE:], z64, g[NOPE:][_ROT_PERM], z64])[None]


def _pad_lanes(v):
    return jnp.pad(v.reshape(1, -1), ((0, 0), (0, V7X_LANES - v.size)))


def kernel(x, c, ctx, c_ctx, w_mod, b_mod, norm_mix, norm_ffn, w_in, q_a_norm, w_uq, kv_a_norm, w_ukv, q_norm, k_norm, attn_out_norm, sgu_norm, sgu_w, sgu_b, gmlp_out_norm, ssd_conv_w, ssd_conv_b, ssd_dt_bias, ssd_a_log, ssd_d, ssd_norm, w_out, ffn_w_gate, ffn_w_up, ffn_conv_w, ffn_conv_b, ffn_w_down):
    batch, seq, d = x.shape
    n_ctx = ctx.shape[1]
    depth = w_mod.shape[0]
    tm = min(512, seq)
    tm_ctx = min(256, n_ctx)
    tf = 512
    tk_att = 1024

    cc = jnp.concatenate([c, c_ctx[None], jnp.zeros((V7X_SUBLANES - batch - 1, d), F32)], axis=0)
    mods = _mods(cc, w_mod, b_mod)

    cs_lat = _rope_tables(seq)
    cs_ctx = jnp.concatenate([jnp.ones((n_ctx, ROPE), F32), jnp.zeros((n_ctx, 2 * V7X_LANES - ROPE), F32)], axis=-1)

    ti = np.arange(SSD_Q)
    tri = jnp.asarray(np.stack([ti[:, None] >= ti[None, :], ti[:, None] <= ti[None, :]]), BF16)
    lane_head = np.arange(SSD_IN) // SSD_P
    col = np.arange(V7X_LANES)
    expand = jnp.asarray(np.stack([col[:, None] == lane_head[None, :],
                                   col[:, None] == lane_head[None, :] + SSD_H]), BF16)

    x_lat, x_ctx = x, ctx
    for l in range(depth):
        need_ctx = l < depth - 1
        m = mods[l].reshape(V7X_SUBLANES, 6, d)
        pad = jnp.zeros((V7X_SUBLANES - 6, d), F32)
        mod_lat = jnp.stack([jnp.concatenate([m[bi], pad]) for bi in range(batch)])
        mod_ctx = jnp.stack([jnp.concatenate([m[batch], pad])] * batch)

        w_in_p = _pack_w_in(w_in[l])
        w_uq_p = _pack_w_uq(w_uq[l])
        w_ukv_b = w_ukv[l].astype(BF16)
        gq, gk = q_norm[l], k_norm[l]
        gq1, gk1 = gq[None, :NOPE], gk[None, :NOPE]
        gq2, gk2 = _rot_gains(gq), _rot_gains(gk)
        an_q, an_kv = q_a_norm[l][None], kv_a_norm[l][None]
        sgu_w_b = sgu_w[l].astype(BF16)
        sgu_b_b = jnp.broadcast_to(sgu_b[l][:, :, None], (SGU_G, SGU_CH, SGU_CH))
        conv_w, conv_b = ssd_conv_w[l], ssd_conv_b[l][None]
        bias, alog = _pad_lanes(ssd_dt_bias[l]), _pad_lanes(ssd_a_log[l])
        dvec = jnp.repeat(ssd_d[l], SSD_P)[None]
        w_out_b = w_out[l].astype(BF16)
        wg, wu, wd = ffn_w_gate[l].astype(BF16), ffn_w_up[l].astype(BF16), ffn_w_down[l].astype(BF16)
        fcw, fcb = ffn_conv_w[l], ffn_conv_b[l][None]

        def mixers(xs_in, mod, cs, t_rows, init_state, k_c=None, v_c=None, outputs=True):
            pq, pkv, psgu, pz, pxbc, pdt = _in_proj(xs_in, mod, norm_mix[l][None], w_in_p, t_rows)
            k, v = _kv_proj(pkv, an_kv, w_ukv_b, gk1, gk2, cs, t_rows)
            yf, yb, xs, fin = _ssd(pxbc, pdt, conv_w, conv_b, bias, alog, tri, expand, init_state)
            if not outputs:
                return None, k, v, fin
            q = _q_proj(pq, an_q, w_uq_p, gq1, gq2, cs, t_rows)
            if k_c is None:
                att = _attention(q, k, v, tq=t_rows, tk=tk_att)
            else:
                att = _attention(q, k_c, v_c, k, v, tq=t_rows, tk=tk_att)
            sgu = _sgu(psgu, sgu_norm[l][None], sgu_w_b, sgu_b_b, t_rows)
            out = _out_proj(xs_in, mod, att, sgu, yf, yb, xs, pz, attn_out_norm[l][None],
                            gmlp_out_norm[l][None], dvec, ssd_norm[l][None], w_out_b, t_rows)
            return out, k, v, fin

        zero_state = jnp.zeros((batch, 2, SSD_G, SSD_N, GRP_W), F32)
        x_ctx_mid, k_c, v_c, ctx_state = mixers(x_ctx, mod_ctx, cs_ctx, tm_ctx, zero_state, outputs=need_ctx)
        x_lat, _, _, _ = mixers(x_lat, mod_lat, cs_lat, tm, ctx_state, k_c, v_c)
        x_lat = _ffn(x_lat, mod_lat, norm_ffn[l][None], wg, wu, fcw, fcb, wd, tm, tf)
        if need_ctx:
            x_ctx = _ffn(x_ctx_mid, mod_ctx, norm_ffn[l][None], wg, wu, fcw, fcb, wd, tm_ctx, tf)
    return x_lat
```

```python
import functools
import math

import jax
import jax.numpy as jnp
import numpy as np
from jax import lax
from jax.experimental import pallas as pl
from jax.experimental.pallas import tpu as pltpu

F32 = jnp.float32
BF16 = jnp.bfloat16
NORM_EPS = 1e-6

GRID_COLS = 64
N_HEADS = 8
NOPE = 128
ROPE = 64
QK = NOPE + ROPE
V_DIM = 128
Q_LORA = 768
KV_LORA = 512
ROPE_THETA = 10000.0
SM_SCALE = 1.0 / math.sqrt(QK)
LOG2_E = math.log2(math.e)
SCORE_BOUND_LIMIT = 60.0
HEAD_PAD = 256
ATT_W = N_HEADS * V_DIM
SGU_G = 4
SGU_CH = 128
SGU_W = 512
SSD_H = 8
SSD_P = 64
SSD_IN = SSD_H * SSD_P
SSD_N = 128
SSD_G = 2
SSD_Q = 128
XBC_W = SSD_IN + 2 * SSD_G * SSD_N
GRP_W = (SSD_H // SSD_G) * SSD_P

V7X_LANES = 128
V7X_SUBLANES = 8
BF16_ROWS = 2 * V7X_SUBLANES
V7X_VMEM_BYTES = 64 * 1024 * 1024
VMEM_LIMIT = V7X_VMEM_BYTES - 8 * 1024 * 1024

SEG_Q = (0, Q_LORA)
SEG_KV = (SEG_Q[1], SEG_Q[1] + KV_LORA + 2 * V7X_LANES)
SEG_SGU = (SEG_KV[1], SEG_KV[1] + 2 * SGU_W)
SEG_Z = (SEG_SGU[1], SEG_SGU[1] + SSD_IN)
SEG_XBC = (SEG_Z[1], SEG_Z[1] + XBC_W)
SEG_DT = (SEG_XBC[1], SEG_XBC[1] + V7X_LANES)
P_PAD = SEG_DT[1]
Q_HEAD_COLS = NOPE + 2 * V7X_LANES

_ROT_PERM = np.concatenate([np.arange(16, 32), np.arange(0, 16), np.arange(48, 64), np.arange(32, 48)])


def _params(sem, vmem=VMEM_LIMIT):
    return pltpu.CompilerParams(dimension_semantics=sem, vmem_limit_bytes=vmem)


def _dot(a, b):
    return jnp.dot(a, b, preferred_element_type=F32)


def _dot_nt(a, b):
    return lax.dot_general(a, b, (((1,), (1,)), ((), ())), preferred_element_type=F32)


def _rms(t, gain):
    return t * lax.rsqrt(jnp.mean(t * t, axis=-1, keepdims=True) + NORM_EPS) * gain


def _silu(t):
    return t * (1.0 / (1.0 + jnp.exp(-t)))


def _gelu_tanh(t):
    return 0.5 * t * (1.0 + jnp.tanh(math.sqrt(2.0 / math.pi) * (t + 0.044715 * (t * t * t))))


def _softplus(t):
    return jnp.maximum(t, 0.0) + jnp.log1p(jnp.exp(-jnp.abs(t)))


def _split3(t):
    hi = t.astype(BF16)
    r1 = t - hi.astype(F32)
    mid = r1.astype(BF16)
    lo = (r1 - mid.astype(F32)).astype(BF16)
    return hi, mid, lo


def _mods_kernel(c_ref, w_ref, b_ref, o_ref):
    s = _silu(c_ref[...])
    s_hi = s.astype(BF16)
    s_lo = (s - s_hi.astype(F32)).astype(BF16)
    w = w_ref[0]
    w_hi = w.astype(BF16)
    w_lo = (w - w_hi.astype(F32)).astype(BF16)
    o_ref[0] = _dot(s_hi, w_hi) + _dot(s_lo, w_hi) + _dot(s_hi, w_lo) + b_ref[0]


def _mods(cc, w_mod, b_mod):
    depth, d, n6 = w_mod.shape
    tn = 1024
    return pl.pallas_call(
        _mods_kernel,
        out_shape=jax.ShapeDtypeStruct((depth, V7X_SUBLANES, n6), F32),
        grid=(depth, n6 // tn),
        in_specs=[pl.BlockSpec((V7X_SUBLANES, d), lambda l, j: (0, 0)),
                  pl.BlockSpec((1, d, tn), lambda l, j: (l, 0, j)),
                  pl.BlockSpec((1, 1, tn), lambda l, j: (l, 0, j))],
        out_specs=pl.BlockSpec((1, V7X_SUBLANES, tn), lambda l, j: (l, 0, j)),
        compiler_params=_params(("parallel", "parallel")),
        name="mods",
    )(cc, w_mod, b_mod.reshape(depth, 1, n6))


def _in_proj_kernel(x_ref, mod_ref, g_ref, w_ref, q_ref, kv_ref, sgu_ref, z_ref, xbc_ref, dt_ref):
    x = x_ref[0]
    shift = mod_ref[0, 0:1, :]
    scale = mod_ref[0, 1:2, :]
    h = (_rms(x, g_ref[...]) * (1.0 + scale) + shift).astype(BF16)
    for ref, (a, b) in ((q_ref, SEG_Q), (kv_ref, SEG_KV), (sgu_ref, SEG_SGU), (z_ref, SEG_Z),
                        (xbc_ref, SEG_XBC), (dt_ref, SEG_DT)):
        ref[0] = _dot(h, w_ref[:, a:b]).astype(ref.dtype)


def _in_proj(x, mod, gain, w_in_p, layer, tm):
    b, s, d = x.shape
    segs = (SEG_Q, SEG_KV, SEG_SGU, SEG_Z, SEG_XBC, SEG_DT)
    dts = (BF16, BF16, BF16, BF16, BF16, F32)
    return pl.pallas_call(
        _in_proj_kernel,
        out_shape=[jax.ShapeDtypeStruct((b, s, hi - lo), dt) for (lo, hi), dt in zip(segs, dts)],
        grid=(b, s // tm),
        in_specs=[pl.BlockSpec((1, tm, d), lambda bi, i: (bi, i, 0)),
                  pl.BlockSpec((1, V7X_SUBLANES, d), lambda bi, i: (bi, 0, 0)),
                  pl.BlockSpec((1, d), lambda bi, i: (0, 0)),
                  pl.BlockSpec((None, d, P_PAD), lambda bi, i: (layer, 0, 0), pipeline_mode=pl.Buffered(1))],
        out_specs=[pl.BlockSpec((1, tm, hi - lo), lambda bi, i: (bi, i, 0)) for lo, hi in segs],
        compiler_params=_params(("parallel", "parallel")),
        name="in_proj",
    )(x, mod, gain, w_in_p)


def _q_proj_kernel(cq_ref, an_ref, w_ref, g_ref, g2_ref, cs_ref, q_ref):
    cq = cq_ref[0].astype(F32)
    cqn = _rms(cq, an_ref[...]).astype(BF16)
    gcs = g2_ref[...] * cs_ref[...]
    gc, gs = gcs[:, :V7X_LANES], gcs[:, V7X_LANES:]
    for h in range(N_HEADS):
        y = _dot(cqn, w_ref[:, h * Q_HEAD_COLS:(h + 1) * Q_HEAD_COLS])
        a, r, rp = y[:, :NOPE], y[:, NOPE:2 * NOPE], y[:, 2 * NOPE:]
        ssq = jnp.sum(a * a + r * r, axis=-1, keepdims=True)
        rs = lax.rsqrt(ssq * (1.0 / QK) + NORM_EPS) * (SM_SCALE * LOG2_E)
        q_ref[0, :, h * HEAD_PAD:h * HEAD_PAD + NOPE] = (a * g_ref[...] * rs).astype(BF16)
        q_ref[0, :, h * HEAD_PAD + NOPE:(h + 1) * HEAD_PAD] = ((r * gc + rp * gs) * rs).astype(BF16)


def _q_proj(cq, an, w_uq_p, layer, g, g2, cs, tm):
    b, s, _ = cq.shape
    width = N_HEADS * HEAD_PAD
    return pl.pallas_call(
        _q_proj_kernel,
        out_shape=jax.ShapeDtypeStruct((b, s, width), BF16),
        grid=(b, s // tm),
        in_specs=[pl.BlockSpec((1, tm, Q_LORA), lambda bi, i: (bi, i, 0)),
                  pl.BlockSpec((1, Q_LORA), lambda bi, i: (0, 0)),
                  pl.BlockSpec((None, Q_LORA, N_HEADS * Q_HEAD_COLS), lambda bi, i: (layer, 0, 0)),
                  pl.BlockSpec((1, NOPE), lambda bi, i: (0, 0)),
                  pl.BlockSpec((1, 2 * V7X_LANES), lambda bi, i: (0, 0)),
                  pl.BlockSpec((tm, 2 * V7X_LANES), lambda bi, i: (i, 0))],
        out_specs=pl.BlockSpec((1, tm, width), lambda bi, i: (bi, i, 0)),
        compiler_params=_params(("parallel", "parallel")),
        name="q_proj",
    )(cq, an, w_uq_p, g, g2, cs)


def _kv_proj_kernel(t_ref, an_ref, w_ref, g_ref, g2_ref, cs_ref, k_ref, v_ref):
    ckv = t_ref[0, :, :KV_LORA].astype(F32)
    kr = t_ref[0, :, KV_LORA:KV_LORA + V7X_LANES].astype(F32)
    krp = t_ref[0, :, KV_LORA + V7X_LANES:].astype(F32)
    ckvn = _rms(ckv, an_ref[...]).astype(BF16)
    gcs = g2_ref[...] * cs_ref[...]
    rot = kr * gcs[:, :V7X_LANES] + krp * gcs[:, V7X_LANES:]
    kr_sq = kr * kr
    lane = lax.broadcasted_iota(jnp.int32, kr.shape, 1)
    ones_col = jnp.where(lane == 0, 1.0, 0.0).astype(BF16)
    for h in range(N_HEADS):
        y = _dot(ckvn, w_ref[:, h * 2 * NOPE:(h + 1) * 2 * NOPE])
        kn = y[:, :NOPE]
        rs = lax.rsqrt(jnp.sum(kn * kn + kr_sq, axis=-1, keepdims=True) * (1.0 / QK) + NORM_EPS)
        k_ref[0, :, h * HEAD_PAD:h * HEAD_PAD + NOPE] = (kn * g_ref[...] * rs).astype(BF16)
        k_ref[0, :, h * HEAD_PAD + NOPE:(h + 1) * HEAD_PAD] = (rot * rs).astype(BF16)
        v_ref[0, :, h * HEAD_PAD:h * HEAD_PAD + V_DIM] = y[:, NOPE:].astype(BF16)
        v_ref[0, :, h * HEAD_PAD + V_DIM:(h + 1) * HEAD_PAD] = ones_col


def _kv_proj(t, an, w_ukv, layer, g, g2, cs, tm):
    b, s, wt = t.shape
    return pl.pallas_call(
        _kv_proj_kernel,
        out_shape=[jax.ShapeDtypeStruct((b, s, N_HEADS * HEAD_PAD), BF16),
                   jax.ShapeDtypeStruct((b, s, N_HEADS * HEAD_PAD), BF16)],
        grid=(b, s // tm),
        in_specs=[pl.BlockSpec((1, tm, wt), lambda bi, i: (bi, i, 0)),
                  pl.BlockSpec((1, KV_LORA), lambda bi, i: (0, 0)),
                  pl.BlockSpec((None, KV_LORA, N_HEADS * 2 * NOPE), lambda bi, i: (layer, 0, 0)),
                  pl.BlockSpec((1, NOPE), lambda bi, i: (0, 0)),
                  pl.BlockSpec((1, 2 * V7X_LANES), lambda bi, i: (0, 0)),
                  pl.BlockSpec((tm, 2 * V7X_LANES), lambda bi, i: (i, 0))],
        out_specs=[pl.BlockSpec((1, tm, N_HEADS * HEAD_PAD), lambda bi, i: (bi, i, 0)),
                   pl.BlockSpec((1, tm, N_HEADS * HEAD_PAD), lambda bi, i: (bi, i, 0))],
        compiler_params=_params(("parallel", "parallel")),
        name="kv_proj",
    )(t, an, w_ukv, g, g2, cs)


def _attn_kernel(*refs, n_lat, tk):
    if n_lat:
        bounded_ref, q_ref, kc_ref, vc_ref, kl_ref, vl_ref, o_ref, s0_ref, s1_ref = refs
    else:
        bounded_ref, q_ref, kc_ref, vc_ref, o_ref = refs
    n_chunks = n_lat // tk if n_lat else 0

    def finish(acc):
        o_ref[0] = (acc[:, :V_DIM] * (1.0 / acc[:, V_DIM:V_DIM + 1])).astype(o_ref.dtype)

    @pl.when(bounded_ref[0] != 0)
    def _():
        q = q_ref[0]
        acc = _dot(jnp.exp2(_dot_nt(q, kc_ref[0])).astype(BF16), vc_ref[0])
        for j in range(n_chunks):
            p = jnp.exp2(_dot_nt(q, kl_ref[0, j * tk:(j + 1) * tk, :])).astype(BF16)
            acc = acc + _dot(p, vl_ref[0, j * tk:(j + 1) * tk, :])
        finish(acc)

    @pl.when(bounded_ref[0] == 0)
    def _():
        q = q_ref[0]
        tq = q.shape[0]

        def update(carry, s, v):
            m, acc = carry
            m_new = jnp.maximum(m, jnp.max(s, axis=-1, keepdims=True))
            p = jnp.exp2(s - m_new).astype(BF16)
            return m_new, jnp.exp2(m - m_new) * acc + _dot(p, v)

        carry = (jnp.full((tq, 1), -jnp.inf, F32), jnp.zeros((tq, HEAD_PAD), F32))
        if n_lat:
            slots = (s0_ref, s1_ref)

            def scores(j):
                slots[j % 2][...] = _dot_nt(q, kl_ref[0, j * tk:(j + 1) * tk, :])

            scores(0)
            carry = update(carry, _dot_nt(q, kc_ref[0]), vc_ref[0])
            for j in range(n_chunks):
                if j + 1 < n_chunks:
                    scores(j + 1)
                carry = update(carry, slots[j % 2][...], vl_ref[0, j * tk:(j + 1) * tk, :])
        else:
            carry = update(carry, _dot_nt(q, kc_ref[0]), vc_ref[0])
        finish(carry[1])


def _attention(bounded, q, k_c, v_c, k_l=None, v_l=None, *, tq, tk):
    b, s, _ = q.shape
    n_ctx = k_c.shape[1]
    n_lat = 0 if k_l is None else k_l.shape[1]
    in_specs = [pl.BlockSpec(memory_space=pltpu.SMEM),
                pl.BlockSpec((1, tq, HEAD_PAD), lambda bi, h, i: (bi, i, h)),
                pl.BlockSpec((1, n_ctx, HEAD_PAD), lambda bi, h, i: (bi, 0, h)),
                pl.BlockSpec((1, n_ctx, HEAD_PAD), lambda bi, h, i: (bi, 0, h))]
    args = [bounded, q, k_c, v_c]
    scratch = []
    if n_lat:
        tk = min(tk, n_lat)
        assert n_lat % tk == 0
        in_specs += [pl.BlockSpec((1, n_lat, HEAD_PAD), lambda bi, h, i: (bi, 0, h)),
                     pl.BlockSpec((1, n_lat, HEAD_PAD), lambda bi, h, i: (bi, 0, h))]
        args += [k_l, v_l]
        scratch = [pltpu.VMEM((tq, tk), F32), pltpu.VMEM((tq, tk), F32)]
    return pl.pallas_call(
        functools.partial(_attn_kernel, n_lat=n_lat, tk=tk),
        out_shape=jax.ShapeDtypeStruct((b, s, ATT_W), BF16),
        grid=(b, N_HEADS, s // tq),
        in_specs=in_specs,
        out_specs=pl.BlockSpec((1, tq, V_DIM), lambda bi, h, i: (bi, i, h)),
        scratch_shapes=scratch,
        compiler_params=_params(("parallel", "parallel", "arbitrary")),
        name="attention",
    )(*args)


def _sgu_kernel(p_ref, gn_ref, w_ref, b_ref, o_ref, *, n_chunks):
    for c in range(n_chunks):
        rows = slice(c * SGU_CH, (c + 1) * SGU_CH)
        z = _gelu_tanh(p_ref[0, rows, :].astype(F32))
        u = z[:, :SGU_W]
        vn = _rms(z[:, SGU_W:], gn_ref[...]).astype(BF16)
        for g in range(SGU_G):
            cols = slice(g * SGU_CH, (g + 1) * SGU_CH)
            mixed = _dot(w_ref[g], vn[:, cols]) + b_ref[g]
            o_ref[0, rows, cols] = (u[:, cols] * mixed).astype(o_ref.dtype)


def _sgu(p, gn, w_s, b_b, rows):
    b, s, _ = p.shape
    return pl.pallas_call(
        functools.partial(_sgu_kernel, n_chunks=rows // SGU_CH),
        out_shape=jax.ShapeDtypeStruct((b, s, SGU_W), BF16),
        grid=(b, s // rows),
        in_specs=[pl.BlockSpec((1, rows, 2 * SGU_W), lambda bi, i: (bi, i, 0)),
                  pl.BlockSpec((1, SGU_W), lambda bi, i: (0, 0)),
                  pl.BlockSpec((SGU_G, SGU_CH, SGU_CH), lambda bi, i: (0, 0, 0)),
                  pl.BlockSpec((SGU_G, SGU_CH, SGU_CH), lambda bi, i: (0, 0, 0))],
        out_specs=pl.BlockSpec((1, rows, SGU_W), lambda bi, i: (bi, i, 0)),
        compiler_params=_params(("parallel", "parallel")),
        name="sgu",
    )(p, gn, w_s, b_b)


def _ssd_direction(xm_ref, xp_ref, xn_ref, dt_ref, chunk, n_chunks, cw_ref, cb_ref, bias_ref, alog_ref,
                   tri_ref, exp_ref, state_ref, y_ref, xs_ref, *, reverse):
    q = SSD_Q
    x = xm_ref[0].astype(F32)
    row = lax.broadcasted_iota(jnp.int32, x.shape, 0)
    prev_row = jnp.where(chunk > 0, xp_ref[0, V7X_SUBLANES - 1:V7X_SUBLANES, :].astype(F32), 0.0)
    next_row = jnp.where(chunk < n_chunks - 1, xn_ref[0, 0:1, :].astype(F32), 0.0)
    up = jnp.where(row == 0, prev_row, pltpu.roll(x, 1, axis=0))
    dn = jnp.where(row == q - 1, next_row, pltpu.roll(x, q - 1, axis=0))
    xbc = _silu(cb_ref[...] + cw_ref[0:1, :] * up + cw_ref[1:2, :] * x + cw_ref[2:3, :] * dn)
    xs = xbc[:, :SSD_IN]
    if xs_ref is not None:
        xs_ref[0] = xs.astype(xs_ref.dtype)

    dt = _softplus(dt_ref[0] + bias_ref[...])
    a_dt = dt * (-jnp.exp(alog_ref[...]))
    tri = tri_ref[...]
    acs = sum(_dot(tri, part) for part in _split3(a_dt))
    acs_t = acs.T
    expand = exp_ref[...]
    acs_x = sum(_dot(part, expand) for part in _split3(acs))
    dt_hi = dt.astype(BF16)
    dt_x = _dot(dt_hi, expand) + _dot((dt - dt_hi.astype(F32)).astype(BF16), expand)
    last = acs_x[0:1, :] if reverse else acs_x[q - 1:q, :]
    xdt = xs * dt_x
    xdt_b = xdt.astype(BF16)
    x_end = (xdt * jnp.exp(last - acs_x)).astype(BF16)
    decay_out = jnp.exp(acs_x)
    chunk_decay = jnp.exp(last)

    ti = lax.broadcasted_iota(jnp.int32, (q, q), 0)
    tj = lax.broadcasted_iota(jnp.int32, (q, q), 1)
    keep = (ti <= tj) if reverse else (ti >= tj)
    lane_blk = lax.shift_right_logical(lax.broadcasted_iota(jnp.int32, (q, GRP_W), 1), int(math.log2(SSD_P)))
    head0 = SSD_H if reverse else 0
    for g in range(SSD_G):
        bm = xbc[:, SSD_IN + g * SSD_N:SSD_IN + (g + 1) * SSD_N]
        cm = xbc[:, SSD_IN + (SSD_G + g) * SSD_N:SSD_IN + (SSD_G + g + 1) * SSD_N].astype(BF16)
        cb = _dot_nt(cm, bm.astype(BF16))
        cols = slice(g * GRP_W, (g + 1) * GRP_W)
        state = state_ref[g]
        y = _dot(cm, state.astype(BF16)) * decay_out[:, cols]
        state_ref[g] = state * chunk_decay[:, cols] + _dot(bm.T.astype(BF16), x_end[:, cols])
        for hh in range(SSD_H // SSD_G):
            hcol = head0 + g * (SSD_H // SSD_G) + hh
            seg = acs[:, hcol:hcol + 1] - acs_t[hcol:hcol + 1, :]
            m = (cb * jnp.exp(jnp.where(keep, seg, -jnp.inf))).astype(BF16)
            y = y + jnp.where(lane_blk == hh, _dot(m, xdt_b[:, cols]), 0.0)
        y_ref[0, :, cols] = y.astype(y_ref.dtype)


def _ssd_kernel(xf_ref, xfp_ref, xfn_ref, dtf_ref, xb_ref, xbp_ref, xbn_ref, dtb_ref,
                cw_ref, cb_ref, bias_ref, alog_ref, tri_ref, exp_ref, init_ref,
                yf_ref, yb_ref, xs_ref, fin_ref, state_ref):
    s = pl.program_id(1)
    n_chunks = pl.num_programs(1)

    @pl.when(s == 0)
    def _():
        state_ref[...] = init_ref[0]

    common = (cw_ref, cb_ref, bias_ref, alog_ref)
    _ssd_direction(xf_ref, xfp_ref, xfn_ref, dtf_ref, s, n_chunks, *common, tri_ref.at[0], exp_ref.at[0],
                   state_ref.at[0], yf_ref, xs_ref, reverse=False)
    _ssd_direction(xb_ref, xbp_ref, xbn_ref, dtb_ref, n_chunks - 1 - s, n_chunks, *common, tri_ref.at[1],
                   exp_ref.at[1], state_ref.at[1], yb_ref, None, reverse=True)

    @pl.when(s == n_chunks - 1)
    def _():
        fin_ref[0] = state_ref[...]


def _ssd(xbc, dt, conv_w, conv_b, bias, alog, tri, expand, init):
    b, s, _ = xbc.shape
    nc = s // SSD_Q
    per = SSD_Q // V7X_SUBLANES
    last_blk = s // V7X_SUBLANES - 1

    def main_f(bi, i): return (bi, i, 0)
    def prev_f(bi, i): return (bi, jnp.maximum(i * per - 1, 0), 0)
    def next_f(bi, i): return (bi, jnp.minimum((i + 1) * per, last_blk), 0)
    def main_b(bi, i): return (bi, nc - 1 - i, 0)
    def prev_b(bi, i): return (bi, jnp.maximum((nc - 1 - i) * per - 1, 0), 0)
    def next_b(bi, i): return (bi, jnp.minimum((nc - i) * per, last_blk), 0)
    const2 = lambda bi, i: (0, 0)
    const3 = lambda bi, i: (0, 0, 0)
    state_shape = (2, SSD_G, SSD_N, GRP_W)
    return pl.pallas_call(
        _ssd_kernel,
        out_shape=[jax.ShapeDtypeStruct((b, s, SSD_IN), BF16), jax.ShapeDtypeStruct((b, s, SSD_IN), BF16),
                   jax.ShapeDtypeStruct((b, s, SSD_IN), BF16), jax.ShapeDtypeStruct((b,) + state_shape, F32)],
        grid=(b, nc),
        in_specs=[pl.BlockSpec((1, SSD_Q, XBC_W), main_f), pl.BlockSpec((1, V7X_SUBLANES, XBC_W), prev_f),
                  pl.BlockSpec((1, V7X_SUBLANES, XBC_W), next_f), pl.BlockSpec((1, SSD_Q, V7X_LANES), main_f),
                  pl.BlockSpec((1, SSD_Q, XBC_W), main_b), pl.BlockSpec((1, V7X_SUBLANES, XBC_W), prev_b),
                  pl.BlockSpec((1, V7X_SUBLANES, XBC_W), next_b), pl.BlockSpec((1, SSD_Q, V7X_LANES), main_b),
                  pl.BlockSpec((3, XBC_W), const2), pl.BlockSpec((1, XBC_W), const2),
                  pl.BlockSpec((1, V7X_LANES), const2), pl.BlockSpec((1, V7X_LANES), const2),
                  pl.BlockSpec((2, SSD_Q, SSD_Q), const3), pl.BlockSpec((2, V7X_LANES, SSD_IN), const3),
                  pl.BlockSpec((1,) + state_shape, lambda bi, i: (bi, 0, 0, 0, 0))],
        out_specs=[pl.BlockSpec((1, SSD_Q, SSD_IN), main_f), pl.BlockSpec((1, SSD_Q, SSD_IN), main_b),
                   pl.BlockSpec((1, SSD_Q, SSD_IN), main_f),
                   pl.BlockSpec((1,) + state_shape, lambda bi, i: (bi, 0, 0, 0, 0))],
        scratch_shapes=[pltpu.VMEM(state_shape, F32)],
        compiler_params=_params(("parallel", "arbitrary")),
        name="ssd",
    )(xbc, xbc, xbc, dt, xbc, xbc, xbc, dt, conv_w, conv_b, bias, alog, tri, expand, init)


def _out_proj_kernel(x_ref, mod_ref, att_ref, sgu_ref, yf_ref, yb_ref, xs_ref, z_ref,
                     ga_ref, gs_ref, d_ref, gy_ref, gf_ref, w_ref, o_ref, h_ref):
    att = _rms(att_ref[0].astype(F32), ga_ref[...]).astype(BF16)
    sgu = _rms(sgu_ref[0].astype(F32), gs_ref[...]).astype(BF16)
    y = yf_ref[0].astype(F32) + yb_ref[0].astype(F32) + d_ref[...] * xs_ref[0].astype(F32)
    ssd = _rms(y * _silu(z_ref[0].astype(F32)), gy_ref[...]).astype(BF16)
    mix = (_dot(att, w_ref[:ATT_W, :]) + _dot(sgu, w_ref[ATT_W:ATT_W + SGU_W, :])
           + _dot(ssd, w_ref[ATT_W + SGU_W:, :]))
    x_new = x_ref[0] + mod_ref[0, 2:3, :] * mix
    o_ref[0] = x_new
    h_ref[0] = (_rms(x_new, gf_ref[...]) * (1.0 + mod_ref[0, 4:5, :]) + mod_ref[0, 3:4, :]).astype(BF16)


def _out_proj(x, mod, att, sgu, yf, yb, xs, z, ga, gs, dvec, gy, gf, w_out, layer, tm):
    b, s, d = x.shape
    row = lambda w: pl.BlockSpec((1, tm, w), lambda bi, i: (bi, i, 0))
    vec = lambda w: pl.BlockSpec((1, w), lambda bi, i: (0, 0))
    return pl.pallas_call(
        _out_proj_kernel,
        out_shape=[jax.ShapeDtypeStruct((b, s, d), F32), jax.ShapeDtypeStruct((b, s, d), BF16)],
        grid=(b, s // tm),
        in_specs=[row(d), pl.BlockSpec((1, V7X_SUBLANES, d), lambda bi, i: (bi, 0, 0)),
                  row(ATT_W), row(SGU_W), row(SSD_IN), row(SSD_IN), row(SSD_IN), row(SSD_IN),
                  vec(ATT_W), vec(SGU_W), vec(SSD_IN), vec(SSD_IN), vec(d),
                  pl.BlockSpec((None,) + w_out.shape[1:], lambda bi, i: (layer, 0, 0),
                               pipeline_mode=pl.Buffered(1))],
        out_specs=[row(d), row(d)],
        compiler_params=_params(("parallel", "parallel")),
        name="out_proj",
    )(x, mod, att, sgu, yf, yb, xs, z, ga, gs, dvec, gy, gf, w_out)


def _ffn_kernel(x_ref, h_ref, hp_ref, hn_ref, mod_ref, wg_ref, wu_ref, cw_ref, cb_ref, wd_ref, o_ref,
                halo_sc, acc_sc):
    i = pl.program_id(1)
    j = pl.program_id(2)
    hb = BF16_ROWS

    @pl.when(j == 0)
    def _():
        halo_sc[0:hb, :] = jnp.where(i > 0, hp_ref[0], jnp.zeros_like(hp_ref[0]))
        halo_sc[hb:, :] = jnp.where(i < pl.num_programs(1) - 1, hn_ref[0], jnp.zeros_like(hn_ref[0]))
        acc_sc[...] = jnp.zeros_like(acc_sc)

    h = h_ref[0]
    tm = h.shape[0]
    gate = _dot(h, wg_ref[...])
    gate_halo = _dot(halo_sc[...], wg_ref[...])
    row = lax.broadcasted_iota(jnp.int32, gate.shape, 0)
    up = jnp.where(row == 0, gate_halo[hb - 1:hb, :], pltpu.roll(gate, 1, axis=0))
    dn = jnp.where(row == tm - 1, gate_halo[hb:hb + 1, :], pltpu.roll(gate, tm - 1, axis=0))
    conv = cb_ref[...] + cw_ref[0:1, :] * up + cw_ref[1:2, :] * gate + cw_ref[2:3, :] * dn
    act = (_silu(conv) * _dot(h, wu_ref[...])).astype(BF16)
    acc_sc[...] += _dot(act, wd_ref[...])

    @pl.when(j == pl.num_programs(2) - 1)
    def _():
        o_ref[0] = x_ref[0] + mod_ref[0, 5:6, :] * acc_sc[...]


def _ffn(x, h, mod, w_gate, w_up, conv_w, conv_b, w_down, layer, tm, tf):
    b, s, d = x.shape
    f = w_down.shape[1]
    per = tm // BF16_ROWS
    last_blk = s // BF16_ROWS - 1
    assert f % tf == 0
    cur = lambda bi, i, j: (0, j)
    return pl.pallas_call(
        _ffn_kernel,
        out_shape=jax.ShapeDtypeStruct((b, s, d), F32),
        grid=(b, s // tm, f // tf),
        in_specs=[pl.BlockSpec((1, tm, d), lambda bi, i, j: (bi, i, 0)),
                  pl.BlockSpec((1, tm, d), lambda bi, i, j: (bi, i, 0)),
                  pl.BlockSpec((1, BF16_ROWS, d), lambda bi, i, j: (bi, jnp.maximum(i * per - 1, 0), 0)),
                  pl.BlockSpec((1, BF16_ROWS, d), lambda bi, i, j: (bi, jnp.minimum((i + 1) * per, last_blk), 0)),
                  pl.BlockSpec((1, V7X_SUBLANES, d), lambda bi, i, j: (bi, 0, 0)),
                  pl.BlockSpec((None, d, tf), lambda bi, i, j: (layer, 0, j)),
                  pl.BlockSpec((None, d, tf), lambda bi, i, j: (layer, 0, j)),
                  pl.BlockSpec((3, tf), cur),
                  pl.BlockSpec((1, tf), cur),
                  pl.BlockSpec((None, tf, d), lambda bi, i, j: (layer, j, 0))],
        out_specs=pl.BlockSpec((1, tm, d), lambda bi, i, j: (bi, i, 0)),
        scratch_shapes=[pltpu.VMEM((2 * BF16_ROWS, d), BF16), pltpu.VMEM((tm, d), F32)],
        compiler_params=_params(("parallel", "parallel", "arbitrary")),
        name="ffn",
    )(x, h, h, h, mod, w_gate, w_up, conv_w, conv_b, w_down)


def _rope_tables(seq):
    rows = seq // GRID_COLS
    pairs = ROPE // 4
    freqs = ROPE_THETA ** (-jnp.arange(pairs, dtype=F32) / pairs)
    ar = jnp.arange(rows, dtype=F32)[:, None] * freqs
    ac = jnp.arange(GRID_COLS, dtype=F32)[:, None] * freqs
    by_row = lambda t: jnp.repeat(t, GRID_COLS, axis=0)
    by_col = lambda t: jnp.tile(t, (rows, 1))
    cr, sr, cc, sc = by_row(jnp.cos(ar)), by_row(jnp.sin(ar)), by_col(jnp.cos(ac)), by_col(jnp.sin(ac))
    cos = jnp.concatenate([cr, cr, cc, cc], axis=-1)
    sin = jnp.concatenate([-sr, sr, -sc, sc], axis=-1)
    zeros = jnp.zeros_like(cos)
    return jnp.concatenate([cos, zeros, sin, zeros], axis=-1)


def _pack_w_in(w):
    w = w.astype(BF16)
    off_kr, off_sgu = Q_LORA + KV_LORA, Q_LORA + KV_LORA + ROPE
    off_dt = off_sgu + 2 * SGU_W + SSD_IN + XBC_W
    kr = w[..., off_kr:off_sgu]
    dt = w[..., off_dt:]
    z64 = jnp.zeros(w.shape[:2] + (V7X_LANES - ROPE,), BF16)
    return jnp.concatenate([w[..., :off_kr], kr, z64, kr[..., _ROT_PERM], z64, w[..., off_sgu:off_dt], dt,
                            jnp.zeros(w.shape[:2] + (V7X_LANES - dt.shape[-1],), BF16)], axis=-1)


def _pack_w_uq(w):
    depth = w.shape[0]
    w = w.astype(BF16).reshape(depth, Q_LORA, N_HEADS, QK)
    rot = w[..., NOPE:]
    z64 = jnp.zeros((depth, Q_LORA, N_HEADS, V7X_LANES - ROPE), BF16)
    return jnp.concatenate([w, z64, rot[..., _ROT_PERM], z64], axis=-1).reshape(
        depth, Q_LORA, N_HEADS * Q_HEAD_COLS)


def _rot_gains(g):
    z64 = jnp.zeros((V7X_LANES - ROPE,), g.dtype)
    return jnp.concatenate([g[NOPE:], z64, g[NOPE:][_ROT_PERM], z64])[None]


def _pad_lanes(v):
    return jnp.pad(v.reshape(1, -1), ((0, 0), (0, V7X_LANES - v.size)))


def kernel(x, c, ctx, c_ctx, w_mod, b_mod, norm_mix, norm_ffn, w_in, q_a_norm, w_uq, kv_a_norm, w_ukv, q_norm, k_norm, attn_out_norm, sgu_norm, sgu_w, sgu_b, gmlp_out_norm, ssd_conv_w, ssd_conv_b, ssd_dt_bias, ssd_a_log, ssd_d, ssd_norm, w_out, ffn_w_gate, ffn_w_up, ffn_conv_w, ffn_conv_b, ffn_w_down):
    batch, seq, d = x.shape
    n_ctx = ctx.shape[1]
    depth = w_mod.shape[0]
    tm = min(512, seq)
    tm_ctx = min(256, n_ctx)
    tf = 512
    tk_att = 1024
    tq_att = 1024

    cc = jnp.concatenate([c, c_ctx[None], jnp.zeros((V7X_SUBLANES - batch - 1, d), F32)], axis=0)
    mods = _mods(cc, w_mod, b_mod)

    cs_lat = _rope_tables(seq)
    cs_ctx = jnp.concatenate([jnp.ones((n_ctx, ROPE), F32), jnp.zeros((n_ctx, 2 * V7X_LANES - ROPE), F32)], axis=-1)

    ti = np.arange(SSD_Q)
    tri = jnp.asarray(np.stack([ti[:, None] >= ti[None, :], ti[:, None] <= ti[None, :]]), BF16)
    lane_head = np.arange(SSD_IN) // SSD_P
    col = np.arange(V7X_LANES)
    expand = jnp.asarray(np.stack([col[:, None] == lane_head[None, :],
                                   col[:, None] == lane_head[None, :] + SSD_H]), BF16)

    w_in_p, w_uq_p = _pack_w_in(w_in), _pack_w_uq(w_uq)
    w_ukv_b, w_out_b = w_ukv.astype(BF16), w_out.astype(BF16)
    wg, wu, wd = ffn_w_gate.astype(BF16), ffn_w_up.astype(BF16), ffn_w_down.astype(BF16)

    x_lat, x_ctx = x, ctx
    for l in range(depth):
        need_ctx = l < depth - 1
        m = mods[l].reshape(V7X_SUBLANES, 6, d)
        pad = jnp.zeros((V7X_SUBLANES - 6, d), F32)
        mod_lat = jnp.stack([jnp.concatenate([m[bi], pad]) for bi in range(batch)])
        mod_ctx = jnp.stack([jnp.concatenate([m[batch], pad])] * batch)

        gq, gk = q_norm[l], k_norm[l]
        gq1, gk1 = gq[None, :NOPE], gk[None, :NOPE]
        gq2, gk2 = _rot_gains(gq), _rot_gains(gk)
        score_bound = QK * jnp.max(jnp.abs(gq)) * jnp.max(jnp.abs(gk)) * (SM_SCALE * LOG2_E)
        bounded = (score_bound <= SCORE_BOUND_LIMIT).astype(jnp.int32).reshape(1)
        an_q, an_kv = q_a_norm[l][None], kv_a_norm[l][None]
        sgu_w_b = sgu_w[l].astype(BF16)
        sgu_b_b = jnp.broadcast_to(sgu_b[l][:, :, None], (SGU_G, SGU_CH, SGU_CH))
        conv_w, conv_b = ssd_conv_w[l], ssd_conv_b[l][None]
        bias, alog = _pad_lanes(ssd_dt_bias[l]), _pad_lanes(ssd_a_log[l])
        dvec = jnp.repeat(ssd_d[l], SSD_P)[None]
        fcw, fcb = ffn_conv_w[l], ffn_conv_b[l][None]

        def mixers(xs_in, mod, cs, t_rows, init_state, k_c=None, v_c=None, outputs=True):
            pq, pkv, psgu, pz, pxbc, pdt = _in_proj(xs_in, mod, norm_mix[l][None], w_in_p, l, t_rows)
            k, v = _kv_proj(pkv, an_kv, w_ukv_b, l, gk1, gk2, cs, t_rows)
            yf, yb, xs, fin = _ssd(pxbc, pdt, conv_w, conv_b, bias, alog, tri, expand, init_state)
            if not outputs:
                return None, k, v, fin
            q = _q_proj(pq, an_q, w_uq_p, l, gq1, gq2, cs, t_rows)
            if k_c is None:
                att = _attention(bounded, q, k, v, tq=t_rows, tk=tk_att)
            else:
                att = _attention(bounded, q, k_c, v_c, k, v, tq=min(tq_att, q.shape[1]), tk=tk_att)
            sgu = _sgu(psgu, sgu_norm[l][None], sgu_w_b, sgu_b_b, t_rows)
            out = _out_proj(xs_in, mod, att, sgu, yf, yb, xs, pz, attn_out_norm[l][None],
                            gmlp_out_norm[l][None], dvec, ssd_norm[l][None], norm_ffn[l][None], w_out_b, l, t_rows)
            return out, k, v, fin

        zero_state = jnp.zeros((batch, 2, SSD_G, SSD_N, GRP_W), F32)
        ctx_mid, k_c, v_c, ctx_state = mixers(x_ctx, mod_ctx, cs_ctx, tm_ctx, zero_state, outputs=need_ctx)
        (x_lat, h_lat), _, _, _ = mixers(x_lat, mod_lat, cs_lat, tm, ctx_state, k_c, v_c)
        x_lat = _ffn(x_lat, h_lat, mod_lat, wg, wu, fcw, fcb, wd, l, tm, tf)
        if need_ctx:
            x_ctx = _ffn(ctx_mid[0], ctx_mid[1], mod_ctx, wg, wu, fcw, fcb, wd, l, tm_ctx, tf)
    return x_lat
```

```python
import functools
import math

import jax
import jax.numpy as jnp
import numpy as np
from jax import lax
from jax.experimental import pallas as pl
from jax.experimental.pallas import tpu as pltpu

F32 = jnp.float32
BF16 = jnp.bfloat16
NORM_EPS = 1e-6

GRID_COLS = 64
N_HEADS = 8
NOPE = 128
ROPE = 64
QK = NOPE + ROPE
V_DIM = 128
Q_LORA = 768
KV_LORA = 512
ROPE_THETA = 10000.0
SM_SCALE = 1.0 / math.sqrt(QK)
LOG2_E = math.log2(math.e)
SCORE_BOUND_LIMIT = 60.0
HEAD_PAD = 256
ATT_W = N_HEADS * V_DIM
SGU_G = 4
SGU_CH = 128
SGU_W = 512
SSD_H = 8
SSD_P = 64
SSD_IN = SSD_H * SSD_P
SSD_N = 128
SSD_G = 2
SSD_Q = 128
XBC_W = SSD_IN + 2 * SSD_G * SSD_N
GRP_W = (SSD_H // SSD_G) * SSD_P

V7X_LANES = 128
V7X_SUBLANES = 8
BF16_ROWS = 2 * V7X_SUBLANES
V7X_VMEM_BYTES = 64 * 1024 * 1024
VMEM_LIMIT = V7X_VMEM_BYTES - 8 * 1024 * 1024

SEG_Q = (0, Q_LORA)
SEG_KV = (SEG_Q[1], SEG_Q[1] + KV_LORA + 2 * V7X_LANES)
SEG_SGU = (SEG_KV[1], SEG_KV[1] + 2 * SGU_W)
SEG_Z = (SEG_SGU[1], SEG_SGU[1] + SSD_IN)
SEG_XBC = (SEG_Z[1], SEG_Z[1] + XBC_W)
SEG_DT = (SEG_XBC[1], SEG_XBC[1] + V7X_LANES)
P_PAD = SEG_DT[1]
Q_HEAD_COLS = NOPE + 2 * V7X_LANES

_ROT_PERM = np.concatenate([np.arange(16, 32), np.arange(0, 16), np.arange(48, 64), np.arange(32, 48)])


def _params(sem, vmem=VMEM_LIMIT):
    return pltpu.CompilerParams(dimension_semantics=sem, vmem_limit_bytes=vmem)


def _dot(a, b):
    return jnp.dot(a, b, preferred_element_type=F32)


def _dot_nt(a, b):
    return lax.dot_general(a, b, (((1,), (1,)), ((), ())), preferred_element_type=F32)


def _rms(t, gain):
    return t * lax.rsqrt(jnp.mean(t * t, axis=-1, keepdims=True) + NORM_EPS) * gain


def _silu(t):
    return t * (1.0 / (1.0 + jnp.exp(-t)))


def _gelu_tanh(t):
    return 0.5 * t * (1.0 + jnp.tanh(math.sqrt(2.0 / math.pi) * (t + 0.044715 * (t * t * t))))


def _softplus(t):
    return jnp.maximum(t, 0.0) + jnp.log1p(jnp.exp(-jnp.abs(t)))


def _split3(t):
    hi = t.astype(BF16)
    r1 = t - hi.astype(F32)
    mid = r1.astype(BF16)
    lo = (r1 - mid.astype(F32)).astype(BF16)
    return hi, mid, lo


def _mods_kernel(c_ref, w_ref, b_ref, o_ref):
    s = _silu(c_ref[...])
    s_hi = s.astype(BF16)
    s_lo = (s - s_hi.astype(F32)).astype(BF16)
    w = w_ref[0]
    w_hi = w.astype(BF16)
    w_lo = (w - w_hi.astype(F32)).astype(BF16)
    o_ref[0] = _dot(s_hi, w_hi) + _dot(s_lo, w_hi) + _dot(s_hi, w_lo) + b_ref[0]


def _mods(cc, w_mod, b_mod):
    depth, d, n6 = w_mod.shape
    tn = 1024
    return pl.pallas_call(
        _mods_kernel,
        out_shape=jax.ShapeDtypeStruct((depth, V7X_SUBLANES, n6), F32),
        grid=(depth, n6 // tn),
        in_specs=[pl.BlockSpec((V7X_SUBLANES, d), lambda l, j: (0, 0)),
                  pl.BlockSpec((1, d, tn), lambda l, j: (l, 0, j)),
                  pl.BlockSpec((1, 1, tn), lambda l, j: (l, 0, j))],
        out_specs=pl.BlockSpec((1, V7X_SUBLANES, tn), lambda l, j: (l, 0, j)),
        compiler_params=_params(("parallel", "parallel")),
        name="mods",
    )(cc, w_mod, b_mod.reshape(depth, 1, n6))


def _in_proj_kernel(x_ref, xp_ref, xn_ref, mod_ref, g_ref, w_ref, cw_ref, cb_ref,
                    q_ref, kv_ref, sgu_ref, z_ref, xbc_ref, dt_ref):
    i = pl.program_id(1)
    shift = mod_ref[0, 0:1, :]
    scale = mod_ref[0, 1:2, :]

    def norm_mod(t):
        return _rms(t, g_ref[...]) * (1.0 + scale) + shift

    a, b = SEG_XBC
    halo = jnp.concatenate([norm_mod(xp_ref[0]), norm_mod(xn_ref[0])], axis=0).astype(BF16)
    pre_halo = _dot(halo, w_ref[:, a:b])
    prev_row = jnp.where(i > 0, pre_halo[V7X_SUBLANES - 1:V7X_SUBLANES, :], 0.0)
    next_row = jnp.where(i < pl.num_programs(1) - 1, pre_halo[V7X_SUBLANES:V7X_SUBLANES + 1, :], 0.0)

    h = norm_mod(x_ref[0]).astype(BF16)
    tm = h.shape[0]
    pre = _dot(h, w_ref[:, a:b])
    row = lax.broadcasted_iota(jnp.int32, pre.shape, 0)
    up = jnp.where(row == 0, prev_row, pltpu.roll(pre, 1, axis=0))
    dn = jnp.where(row == tm - 1, next_row, pltpu.roll(pre, tm - 1, axis=0))
    conv = cb_ref[...] + cw_ref[0:1, :] * up + cw_ref[1:2, :] * pre + cw_ref[2:3, :] * dn
    xbc_ref[0] = _silu(conv).astype(xbc_ref.dtype)

    for ref, (a, b) in ((q_ref, SEG_Q), (kv_ref, SEG_KV), (sgu_ref, SEG_SGU), (z_ref, SEG_Z), (dt_ref, SEG_DT)):
        ref[0] = _dot(h, w_ref[:, a:b]).astype(ref.dtype)


def _in_proj(x, mod, gain, w_in_p, conv_w, conv_b, layer, tm):
    b, s, d = x.shape
    segs = (SEG_Q, SEG_KV, SEG_SGU, SEG_Z, SEG_XBC, SEG_DT)
    dts = (BF16, BF16, BF16, BF16, BF16, F32)
    per = tm // V7X_SUBLANES
    last_blk = s // V7X_SUBLANES - 1
    return pl.pallas_call(
        _in_proj_kernel,
        out_shape=[jax.ShapeDtypeStruct((b, s, hi - lo), dt) for (lo, hi), dt in zip(segs, dts)],
        grid=(b, s // tm),
        in_specs=[pl.BlockSpec((1, tm, d), lambda bi, i: (bi, i, 0)),
                  pl.BlockSpec((1, V7X_SUBLANES, d), lambda bi, i: (bi, jnp.maximum(i * per - 1, 0), 0)),
                  pl.BlockSpec((1, V7X_SUBLANES, d), lambda bi, i: (bi, jnp.minimum((i + 1) * per, last_blk), 0)),
                  pl.BlockSpec((1, V7X_SUBLANES, d), lambda bi, i: (bi, 0, 0)),
                  pl.BlockSpec((1, d), lambda bi, i: (0, 0)),
                  pl.BlockSpec((None, d, P_PAD), lambda bi, i: (layer, 0, 0), pipeline_mode=pl.Buffered(1)),
                  pl.BlockSpec((3, XBC_W), lambda bi, i: (0, 0)),
                  pl.BlockSpec((1, XBC_W), lambda bi, i: (0, 0))],
        out_specs=[pl.BlockSpec((1, tm, hi - lo), lambda bi, i: (bi, i, 0)) for lo, hi in segs],
        compiler_params=_params(("parallel", "parallel")),
        name="in_proj",
    )(x, x, x, mod, gain, w_in_p, conv_w, conv_b)


def _q_proj_kernel(cq_ref, an_ref, w_ref, g_ref, g2_ref, cs_ref, q_ref):
    cq = cq_ref[0].astype(F32)
    cqn = _rms(cq, an_ref[...]).astype(BF16)
    gcs = g2_ref[...] * cs_ref[...]
    gc, gs = gcs[:, :V7X_LANES], gcs[:, V7X_LANES:]
    lanes = V7X_LANES
    for pair in range(N_HEADS // 2):
        y = _dot(cqn, w_ref[:, pair * 2 * Q_HEAD_COLS:(pair + 1) * 2 * Q_HEAD_COLS])
        for e in range(2):
            h = 2 * pair + e
            a = y[:, e * lanes:(e + 1) * lanes]
            r = y[:, (2 + e) * lanes:(3 + e) * lanes]
            rp = y[:, (4 + e) * lanes:(5 + e) * lanes]
            ssq = jnp.sum(a * a + r * r, axis=-1, keepdims=True)
            rs = lax.rsqrt(ssq * (1.0 / QK) + NORM_EPS) * (SM_SCALE * LOG2_E)
            q_ref[0, :, h * HEAD_PAD:h * HEAD_PAD + NOPE] = (a * g_ref[...] * rs).astype(BF16)
            q_ref[0, :, h * HEAD_PAD + NOPE:(h + 1) * HEAD_PAD] = ((r * gc + rp * gs) * rs).astype(BF16)


def _q_proj(cq, an, w_uq_p, layer, g, g2, cs, tm):
    b, s, _ = cq.shape
    width = N_HEADS * HEAD_PAD
    return pl.pallas_call(
        _q_proj_kernel,
        out_shape=jax.ShapeDtypeStruct((b, s, width), BF16),
        grid=(b, s // tm),
        in_specs=[pl.BlockSpec((1, tm, Q_LORA), lambda bi, i: (bi, i, 0)),
                  pl.BlockSpec((1, Q_LORA), lambda bi, i: (0, 0)),
                  pl.BlockSpec((None, Q_LORA, N_HEADS * Q_HEAD_COLS), lambda bi, i: (layer, 0, 0)),
                  pl.BlockSpec((1, NOPE), lambda bi, i: (0, 0)),
                  pl.BlockSpec((1, 2 * V7X_LANES), lambda bi, i: (0, 0)),
                  pl.BlockSpec((tm, 2 * V7X_LANES), lambda bi, i: (i, 0))],
        out_specs=pl.BlockSpec((1, tm, width), lambda bi, i: (bi, i, 0)),
        compiler_params=_params(("parallel", "parallel")),
        name="q_proj",
    )(cq, an, w_uq_p, g, g2, cs)


def _kv_proj_kernel(t_ref, an_ref, w_ref, g_ref, g2_ref, cs_ref, k_ref, v_ref):
    ckv = t_ref[0, :, :KV_LORA].astype(F32)
    kr = t_ref[0, :, KV_LORA:KV_LORA + V7X_LANES].astype(F32)
    krp = t_ref[0, :, KV_LORA + V7X_LANES:].astype(F32)
    ckvn = _rms(ckv, an_ref[...]).astype(BF16)
    gcs = g2_ref[...] * cs_ref[...]
    rot = kr * gcs[:, :V7X_LANES] + krp * gcs[:, V7X_LANES:]
    kr_sq = kr * kr
    lane = lax.broadcasted_iota(jnp.int32, kr.shape, 1)
    ones_col = jnp.where(lane == 0, 1.0, 0.0).astype(BF16)
    for h in range(N_HEADS):
        y = _dot(ckvn, w_ref[:, h * 2 * NOPE:(h + 1) * 2 * NOPE])
        kn = y[:, :NOPE]
        rs = lax.rsqrt(jnp.sum(kn * kn + kr_sq, axis=-1, keepdims=True) * (1.0 / QK) + NORM_EPS)
        k_ref[0, :, h * HEAD_PAD:h * HEAD_PAD + NOPE] = (kn * g_ref[...] * rs).astype(BF16)
        k_ref[0, :, h * HEAD_PAD + NOPE:(h + 1) * HEAD_PAD] = (rot * rs).astype(BF16)
        v_ref[0, :, h * HEAD_PAD:h * HEAD_PAD + V_DIM] = y[:, NOPE:].astype(BF16)
        v_ref[0, :, h * HEAD_PAD + V_DIM:(h + 1) * HEAD_PAD] = ones_col


def _kv_proj(t, an, w_ukv, layer, g, g2, cs, tm):
    b, s, wt = t.shape
    return pl.pallas_call(
        _kv_proj_kernel,
        out_shape=[jax.ShapeDtypeStruct((b, s, N_HEADS * HEAD_PAD), BF16),
                   jax.ShapeDtypeStruct((b, s, N_HEADS * HEAD_PAD), BF16)],
        grid=(b, s // tm),
        in_specs=[pl.BlockSpec((1, tm, wt), lambda bi, i: (bi, i, 0)),
                  pl.BlockSpec((1, KV_LORA), lambda bi, i: (0, 0)),
                  pl.BlockSpec((None, KV_LORA, N_HEADS * 2 * NOPE), lambda bi, i: (layer, 0, 0)),
                  pl.BlockSpec((1, NOPE), lambda bi, i: (0, 0)),
                  pl.BlockSpec((1, 2 * V7X_LANES), lambda bi, i: (0, 0)),
                  pl.BlockSpec((tm, 2 * V7X_LANES), lambda bi, i: (i, 0))],
        out_specs=[pl.BlockSpec((1, tm, N_HEADS * HEAD_PAD), lambda bi, i: (bi, i, 0)),
                   pl.BlockSpec((1, tm, N_HEADS * HEAD_PAD), lambda bi, i: (bi, i, 0))],
        compiler_params=_params(("parallel", "parallel")),
        name="kv_proj",
    )(t, an, w_ukv, g, g2, cs)


def _attn_kernel(*refs, n_lat, tk):
    if n_lat:
        bounded_ref, q_ref, kc_ref, vc_ref, kl_ref, vl_ref, o_ref, s0_ref, s1_ref = refs
    else:
        bounded_ref, q_ref, kc_ref, vc_ref, o_ref = refs
    n_chunks = n_lat // tk if n_lat else 0

    def finish(acc):
        o_ref[0] = (acc[:, :V_DIM] * (1.0 / acc[:, V_DIM:V_DIM + 1])).astype(o_ref.dtype)

    @pl.when(bounded_ref[0] != 0)
    def _():
        q = q_ref[0]
        acc = _dot(jnp.exp2(_dot_nt(q, kc_ref[0])).astype(BF16), vc_ref[0])
        for j in range(n_chunks):
            p = jnp.exp2(_dot_nt(q, kl_ref[0, j * tk:(j + 1) * tk, :])).astype(BF16)
            acc = acc + _dot(p, vl_ref[0, j * tk:(j + 1) * tk, :])
        finish(acc)

    @pl.when(bounded_ref[0] == 0)
    def _():
        q = q_ref[0]
        tq = q.shape[0]

        def update(carry, s, v):
            m, acc = carry
            m_new = jnp.maximum(m, jnp.max(s, axis=-1, keepdims=True))
            p = jnp.exp2(s - m_new).astype(BF16)
            return m_new, jnp.exp2(m - m_new) * acc + _dot(p, v)

        carry = (jnp.full((tq, 1), -jnp.inf, F32), jnp.zeros((tq, HEAD_PAD), F32))
        if n_lat:
            slots = (s0_ref, s1_ref)

            def scores(j):
                slots[j % 2][...] = _dot_nt(q, kl_ref[0, j * tk:(j + 1) * tk, :])

            scores(0)
            carry = update(carry, _dot_nt(q, kc_ref[0]), vc_ref[0])
            for j in range(n_chunks):
                if j + 1 < n_chunks:
                    scores(j + 1)
                carry = update(carry, slots[j % 2][...], vl_ref[0, j * tk:(j + 1) * tk, :])
        else:
            carry = update(carry, _dot_nt(q, kc_ref[0]), vc_ref[0])
        finish(carry[1])


def _attention(bounded, q, k_c, v_c, k_l=None, v_l=None, *, tq, tk):
    b, s, _ = q.shape
    n_ctx = k_c.shape[1]
    n_lat = 0 if k_l is None else k_l.shape[1]
    in_specs = [pl.BlockSpec(memory_space=pltpu.SMEM),
                pl.BlockSpec((1, tq, HEAD_PAD), lambda bi, h, i: (bi, i, h)),
                pl.BlockSpec((1, n_ctx, HEAD_PAD), lambda bi, h, i: (bi, 0, h)),
                pl.BlockSpec((1, n_ctx, HEAD_PAD), lambda bi, h, i: (bi, 0, h))]
    args = [bounded, q, k_c, v_c]
    scratch = []
    if n_lat:
        tk = min(tk, n_lat)
        assert n_lat % tk == 0
        in_specs += [pl.BlockSpec((1, n_lat, HEAD_PAD), lambda bi, h, i: (bi, 0, h)),
                     pl.BlockSpec((1, n_lat, HEAD_PAD), lambda bi, h, i: (bi, 0, h))]
        args += [k_l, v_l]
        scratch = [pltpu.VMEM((tq, tk), F32), pltpu.VMEM((tq, tk), F32)]
    return pl.pallas_call(
        functools.partial(_attn_kernel, n_lat=n_lat, tk=tk),
        out_shape=jax.ShapeDtypeStruct((b, s, ATT_W), BF16),
        grid=(b, N_HEADS, s // tq),
        in_specs=in_specs,
        out_specs=pl.BlockSpec((1, tq, V_DIM), lambda bi, h, i: (bi, i, h)),
        scratch_shapes=scratch,
        compiler_params=_params(("parallel", "parallel", "arbitrary")),
        name="attention",
    )(*args)


def _sgu_kernel(p_ref, gn_ref, w_ref, b_ref, o_ref, *, n_chunks):
    for c in range(n_chunks):
        rows = slice(c * SGU_CH, (c + 1) * SGU_CH)
        z = _gelu_tanh(p_ref[0, rows, :].astype(F32))
        u = z[:, :SGU_W]
        vn = _rms(z[:, SGU_W:], gn_ref[...]).astype(BF16)
        for g in range(SGU_G):
            cols = slice(g * SGU_CH, (g + 1) * SGU_CH)
            mixed = _dot(w_ref[g], vn[:, cols]) + b_ref[g]
            o_ref[0, rows, cols] = (u[:, cols] * mixed).astype(o_ref.dtype)


def _sgu(p, gn, w_s, b_b, rows):
    b, s, _ = p.shape
    return pl.pallas_call(
        functools.partial(_sgu_kernel, n_chunks=rows // SGU_CH),
        out_shape=jax.ShapeDtypeStruct((b, s, SGU_W), BF16),
        grid=(b, s // rows),
        in_specs=[pl.BlockSpec((1, rows, 2 * SGU_W), lambda bi, i: (bi, i, 0)),
                  pl.BlockSpec((1, SGU_W), lambda bi, i: (0, 0)),
                  pl.BlockSpec((SGU_G, SGU_CH, SGU_CH), lambda bi, i: (0, 0, 0)),
                  pl.BlockSpec((SGU_G, SGU_CH, SGU_CH), lambda bi, i: (0, 0, 0))],
        out_specs=pl.BlockSpec((1, rows, SGU_W), lambda bi, i: (bi, i, 0)),
        compiler_params=_params(("parallel", "parallel")),
        name="sgu",
    )(p, gn, w_s, b_b)


def _ssd_direction(x_ref, dt_ref, bias_ref, alog_ref, tri_ref, exp_ref, state_ref, y_ref, *, reverse):
    q = SSD_Q
    xs = x_ref[0, :, :SSD_IN].astype(F32)
    dt = _softplus(dt_ref[0] + bias_ref[...])
    a_dt = dt * (-jnp.exp(alog_ref[...]))
    tri = tri_ref[...]
    acs = sum(_dot(tri, part) for part in _split3(a_dt))
    acs_t = acs.T
    expand = exp_ref[...]
    acs_x = sum(_dot(part, expand) for part in _split3(acs))
    dt_hi = dt.astype(BF16)
    dt_x = _dot(dt_hi, expand) + _dot((dt - dt_hi.astype(F32)).astype(BF16), expand)
    last = acs_x[0:1, :] if reverse else acs_x[q - 1:q, :]
    xdt = xs * dt_x
    x_end = (xdt * jnp.exp(last - acs_x)).astype(BF16)
    decay_out = jnp.exp(acs_x)
    chunk_decay = jnp.exp(last)

    ti = lax.broadcasted_iota(jnp.int32, (q, q), 0)
    tj = lax.broadcasted_iota(jnp.int32, (q, q), 1)
    keep = (ti <= tj) if reverse else (ti >= tj)
    lane_blk = lax.shift_right_logical(lax.broadcasted_iota(jnp.int32, (q, GRP_W), 1), int(math.log2(SSD_P)))
    head0 = SSD_H if reverse else 0
    heads_per_group = SSD_H // SSD_G
    for g in range(SSD_G):
        bm = x_ref[0, :, SSD_IN + g * SSD_N:SSD_IN + (g + 1) * SSD_N]
        cm = x_ref[0, :, SSD_IN + (SSD_G + g) * SSD_N:SSD_IN + (SSD_G + g + 1) * SSD_N]
        cb = _dot_nt(cm, bm)
        cols = slice(g * GRP_W, (g + 1) * GRP_W)
        state = state_ref[g]
        y_off = _dot(cm, state.astype(BF16)) * decay_out[:, cols]
        state_ref[g] = state * chunk_decay[:, cols] + _dot(bm.astype(F32).T.astype(BF16), x_end[:, cols])
        blocks, stacked = [], []
        for hh in range(heads_per_group):
            hcol = head0 + g * heads_per_group + hh
            seg = acs[:, hcol:hcol + 1] - acs_t[hcol:hcol + 1, :]
            blocks.append((cb * jnp.exp(jnp.where(keep, seg, -jnp.inf))).astype(BF16))
            stacked.append(jnp.where(lane_blk == hh, xdt[:, cols], 0.0).astype(BF16))
        y = y_off + _dot(jnp.concatenate(blocks, axis=1), jnp.concatenate(stacked, axis=0))
        y_ref[0, :, cols] = y.astype(y_ref.dtype)


def _ssd_kernel(xf_ref, dtf_ref, xb_ref, dtb_ref, bias_ref, alog_ref, tri_ref, exp_ref, init_ref,
                yf_ref, yb_ref, fin_ref, state_ref):
    s = pl.program_id(1)

    @pl.when(s == 0)
    def _():
        state_ref[...] = init_ref[0]

    _ssd_direction(xf_ref, dtf_ref, bias_ref, alog_ref, tri_ref.at[0], exp_ref.at[0], state_ref.at[0], yf_ref,
                   reverse=False)
    _ssd_direction(xb_ref, dtb_ref, bias_ref, alog_ref, tri_ref.at[1], exp_ref.at[1], state_ref.at[1], yb_ref,
                   reverse=True)

    @pl.when(s == pl.num_programs(1) - 1)
    def _():
        fin_ref[0] = state_ref[...]


def _ssd(xbc, dt, bias, alog, tri, expand, init):
    b, s, _ = xbc.shape
    nc = s // SSD_Q

    def main_f(bi, i): return (bi, i, 0)
    def main_b(bi, i): return (bi, nc - 1 - i, 0)
    const2 = lambda bi, i: (0, 0)
    const3 = lambda bi, i: (0, 0, 0)
    state_shape = (2, SSD_G, SSD_N, GRP_W)
    return pl.pallas_call(
        _ssd_kernel,
        out_shape=[jax.ShapeDtypeStruct((b, s, SSD_IN), BF16), jax.ShapeDtypeStruct((b, s, SSD_IN), BF16),
                   jax.ShapeDtypeStruct((b,) + state_shape, F32)],
        grid=(b, nc),
        in_specs=[pl.BlockSpec((1, SSD_Q, XBC_W), main_f), pl.BlockSpec((1, SSD_Q, V7X_LANES), main_f),
                  pl.BlockSpec((1, SSD_Q, XBC_W), main_b), pl.BlockSpec((1, SSD_Q, V7X_LANES), main_b),
                  pl.BlockSpec((1, V7X_LANES), const2), pl.BlockSpec((1, V7X_LANES), const2),
                  pl.BlockSpec((2, SSD_Q, SSD_Q), const3), pl.BlockSpec((2, V7X_LANES, SSD_IN), const3),
                  pl.BlockSpec((1,) + state_shape, lambda bi, i: (bi, 0, 0, 0, 0))],
        out_specs=[pl.BlockSpec((1, SSD_Q, SSD_IN), main_f), pl.BlockSpec((1, SSD_Q, SSD_IN), main_b),
                   pl.BlockSpec((1,) + state_shape, lambda bi, i: (bi, 0, 0, 0, 0))],
        scratch_shapes=[pltpu.VMEM(state_shape, F32)],
        compiler_params=_params(("parallel", "arbitrary")),
        name="ssd",
    )(xbc, dt, xbc, dt, bias, alog, tri, expand, init)


def _out_proj_kernel(x_ref, mod_ref, att_ref, sgu_ref, yf_ref, yb_ref, xs_ref, z_ref,
                     ga_ref, gs_ref, d_ref, gy_ref, gf_ref, w_ref, o_ref, h_ref):
    att = _rms(att_ref[0].astype(F32), ga_ref[...]).astype(BF16)
    sgu = _rms(sgu_ref[0].astype(F32), gs_ref[...]).astype(BF16)
    y = yf_ref[0].astype(F32) + yb_ref[0].astype(F32) + d_ref[...] * xs_ref[0].astype(F32)
    ssd = _rms(y * _silu(z_ref[0].astype(F32)), gy_ref[...]).astype(BF16)
    mix = (_dot(att, w_ref[:ATT_W, :]) + _dot(sgu, w_ref[ATT_W:ATT_W + SGU_W, :])
           + _dot(ssd, w_ref[ATT_W + SGU_W:, :]))
    x_new = x_ref[0] + mod_ref[0, 2:3, :] * mix
    o_ref[0] = x_new
    h_ref[0] = (_rms(x_new, gf_ref[...]) * (1.0 + mod_ref[0, 4:5, :]) + mod_ref[0, 3:4, :]).astype(BF16)


def _out_proj(x, mod, att, sgu, yf, yb, xs, z, ga, gs, dvec, gy, gf, w_out, layer, tm):
    b, s, d = x.shape
    row = lambda w: pl.BlockSpec((1, tm, w), lambda bi, i: (bi, i, 0))
    vec = lambda w: pl.BlockSpec((1, w), lambda bi, i: (0, 0))
    return pl.pallas_call(
        _out_proj_kernel,
        out_shape=[jax.ShapeDtypeStruct((b, s, d), F32), jax.ShapeDtypeStruct((b, s, d), BF16)],
        grid=(b, s // tm),
        in_specs=[row(d), pl.BlockSpec((1, V7X_SUBLANES, d), lambda bi, i: (bi, 0, 0)),
                  row(ATT_W), row(SGU_W), row(SSD_IN), row(SSD_IN), row(SSD_IN), row(SSD_IN),
                  vec(ATT_W), vec(SGU_W), vec(SSD_IN), vec(SSD_IN), vec(d),
                  pl.BlockSpec((None,) + w_out.shape[1:], lambda bi, i: (layer, 0, 0),
                               pipeline_mode=pl.Buffered(1))],
        out_specs=[row(d), row(d)],
        compiler_params=_params(("parallel", "parallel")),
        name="out_proj",
    )(x, mod, att, sgu, yf, yb, xs, z, ga, gs, dvec, gy, gf, w_out)


def _ffn_kernel(x_ref, h_ref, hp_ref, hn_ref, mod_ref, wg_ref, wu_ref, cw_ref, cb_ref, wd_ref, o_ref,
                halo_sc, acc_sc):
    i = pl.program_id(1)
    j = pl.program_id(2)
    hb = BF16_ROWS

    @pl.when(j == 0)
    def _():
        halo_sc[0:hb, :] = jnp.where(i > 0, hp_ref[0], jnp.zeros_like(hp_ref[0]))
        halo_sc[hb:, :] = jnp.where(i < pl.num_programs(1) - 1, hn_ref[0], jnp.zeros_like(hn_ref[0]))
        acc_sc[...] = jnp.zeros_like(acc_sc)

    h = h_ref[0]
    tm = h.shape[0]
    gate = _dot(h, wg_ref[...])
    gate_halo = _dot(halo_sc[...], wg_ref[...])
    row = lax.broadcasted_iota(jnp.int32, gate.shape, 0)
    up = jnp.where(row == 0, gate_halo[hb - 1:hb, :], pltpu.roll(gate, 1, axis=0))
    dn = jnp.where(row == tm - 1, gate_halo[hb:hb + 1, :], pltpu.roll(gate, tm - 1, axis=0))
    conv = cb_ref[...] + cw_ref[0:1, :] * up + cw_ref[1:2, :] * gate + cw_ref[2:3, :] * dn
    act = (_silu(conv) * _dot(h, wu_ref[...])).astype(BF16)
    acc_sc[...] += _dot(act, wd_ref[...])

    @pl.when(j == pl.num_programs(2) - 1)
    def _():
        o_ref[0] = x_ref[0] + mod_ref[0, 5:6, :] * acc_sc[...]


def _ffn(x, h, mod, w_gate, w_up, conv_w, conv_b, w_down, layer, tm, tf):
    b, s, d = x.shape
    f = w_down.shape[1]
    per = tm // BF16_ROWS
    last_blk = s // BF16_ROWS - 1
    assert f % tf == 0
    cur = lambda bi, i, j: (0, j)
    return pl.pallas_call(
        _ffn_kernel,
        out_shape=jax.ShapeDtypeStruct((b, s, d), F32),
        grid=(b, s // tm, f // tf),
        in_specs=[pl.BlockSpec((1, tm, d), lambda bi, i, j: (bi, i, 0)),
                  pl.BlockSpec((1, tm, d), lambda bi, i, j: (bi, i, 0)),
                  pl.BlockSpec((1, BF16_ROWS, d), lambda bi, i, j: (bi, jnp.maximum(i * per - 1, 0), 0)),
                  pl.BlockSpec((1, BF16_ROWS, d), lambda bi, i, j: (bi, jnp.minimum((i + 1) * per, last_blk), 0)),
                  pl.BlockSpec((1, V7X_SUBLANES, d), lambda bi, i, j: (bi, 0, 0)),
                  pl.BlockSpec((None, d, tf), lambda bi, i, j: (layer, 0, j)),
                  pl.BlockSpec((None, d, tf), lambda bi, i, j: (layer, 0, j)),
                  pl.BlockSpec((3, tf), cur),
                  pl.BlockSpec((1, tf), cur),
                  pl.BlockSpec((None, tf, d), lambda bi, i, j: (layer, j, 0))],
        out_specs=pl.BlockSpec((1, tm, d), lambda bi, i, j: (bi, i, 0)),
        scratch_shapes=[pltpu.VMEM((2 * BF16_ROWS, d), BF16), pltpu.VMEM((tm, d), F32)],
        compiler_params=_params(("parallel", "parallel", "arbitrary")),
        name="ffn",
    )(x, h, h, h, mod, w_gate, w_up, conv_w, conv_b, w_down)


def _rope_tables(seq):
    rows = seq // GRID_COLS
    pairs = ROPE // 4
    freqs = ROPE_THETA ** (-jnp.arange(pairs, dtype=F32) / pairs)
    ar = jnp.arange(rows, dtype=F32)[:, None] * freqs
    ac = jnp.arange(GRID_COLS, dtype=F32)[:, None] * freqs
    by_row = lambda t: jnp.repeat(t, GRID_COLS, axis=0)
    by_col = lambda t: jnp.tile(t, (rows, 1))
    cr, sr, cc, sc = by_row(jnp.cos(ar)), by_row(jnp.sin(ar)), by_col(jnp.cos(ac)), by_col(jnp.sin(ac))
    cos = jnp.concatenate([cr, cr, cc, cc], axis=-1)
    sin = jnp.concatenate([-sr, sr, -sc, sc], axis=-1)
    zeros = jnp.zeros_like(cos)
    return jnp.concatenate([cos, zeros, sin, zeros], axis=-1)


def _pack_w_in(w):
    w = w.astype(BF16)
    off_kr, off_sgu = Q_LORA + KV_LORA, Q_LORA + KV_LORA + ROPE
    off_dt = off_sgu + 2 * SGU_W + SSD_IN + XBC_W
    kr = w[..., off_kr:off_sgu]
    dt = w[..., off_dt:]
    z64 = jnp.zeros(w.shape[:2] + (V7X_LANES - ROPE,), BF16)
    return jnp.concatenate([w[..., :off_kr], kr, z64, kr[..., _ROT_PERM], z64, w[..., off_sgu:off_dt], dt,
                            jnp.zeros(w.shape[:2] + (V7X_LANES - dt.shape[-1],), BF16)], axis=-1)


def _pack_w_uq(w):
    depth = w.shape[0]
    w = w.astype(BF16).reshape(depth, Q_LORA, N_HEADS, QK)
    rot = w[..., NOPE:]
    z64 = jnp.zeros((depth, Q_LORA, N_HEADS, V7X_LANES - ROPE), BF16)
    blocks = jnp.stack([w[..., :NOPE], jnp.concatenate([rot, z64], axis=-1),
                        jnp.concatenate([rot[..., _ROT_PERM], z64], axis=-1)], axis=3)
    blocks = blocks.reshape(depth, Q_LORA, N_HEADS // 2, 2, 3, V7X_LANES).swapaxes(3, 4)
    return blocks.reshape(depth, Q_LORA, N_HEADS * Q_HEAD_COLS)


def _rot_gains(g):
    z64 = jnp.zeros((V7X_LANES - ROPE,), g.dtype)
    return jnp.concatenate([g[NOPE:], z64, g[NOPE:][_ROT_PERM], z64])[None]


def _pad_lanes(v):
    return jnp.pad(v.reshape(1, -1), ((0, 0), (0, V7X_LANES - v.size)))


def kernel(x, c, ctx, c_ctx, w_mod, b_mod, norm_mix, norm_ffn, w_in, q_a_norm, w_uq, kv_a_norm, w_ukv, q_norm, k_norm, attn_out_norm, sgu_norm, sgu_w, sgu_b, gmlp_out_norm, ssd_conv_w, ssd_conv_b, ssd_dt_bias, ssd_a_log, ssd_d, ssd_norm, w_out, ffn_w_gate, ffn_w_up, ffn_conv_w, ffn_conv_b, ffn_w_down):
    batch, seq, d = x.shape
    n_ctx = ctx.shape[1]
    depth = w_mod.shape[0]
    tm = min(512, seq)
    tm_ctx = min(256, n_ctx)
    tf = 512
    tk_att = 1024
    tq_att = 1024

    cc = jnp.concatenate([c, c_ctx[None], jnp.zeros((V7X_SUBLANES - batch - 1, d), F32)], axis=0)
    mods = _mods(cc, w_mod, b_mod)

    cs_lat = _rope_tables(seq)
    cs_ctx = jnp.concatenate([jnp.ones((n_ctx, ROPE), F32), jnp.zeros((n_ctx, 2 * V7X_LANES - ROPE), F32)], axis=-1)

    ti = np.arange(SSD_Q)
    tri = jnp.asarray(np.stack([ti[:, None] >= ti[None, :], ti[:, None] <= ti[None, :]]), BF16)
    lane_head = np.arange(SSD_IN) // SSD_P
    col = np.arange(V7X_LANES)
    expand = jnp.asarray(np.stack([col[:, None] == lane_head[None, :],
                                   col[:, None] == lane_head[None, :] + SSD_H]), BF16)

    w_in_p, w_uq_p = _pack_w_in(w_in), _pack_w_uq(w_uq)
    w_ukv_b, w_out_b = w_ukv.astype(BF16), w_out.astype(BF16)
    wg, wu, wd = ffn_w_gate.astype(BF16), ffn_w_up.astype(BF16), ffn_w_down.astype(BF16)

    x_lat, x_ctx = x, ctx
    for l in range(depth):
        need_ctx = l < depth - 1
        m = mods[l].reshape(V7X_SUBLANES, 6, d)
        pad = jnp.zeros((V7X_SUBLANES - 6, d), F32)
        mod_lat = jnp.stack([jnp.concatenate([m[bi], pad]) for bi in range(batch)])
        mod_ctx = jnp.stack([jnp.concatenate([m[batch], pad])] * batch)

        gq, gk = q_norm[l], k_norm[l]
        gq1, gk1 = gq[None, :NOPE], gk[None, :NOPE]
        gq2, gk2 = _rot_gains(gq), _rot_gains(gk)
        score_bound = QK * jnp.max(jnp.abs(gq)) * jnp.max(jnp.abs(gk)) * (SM_SCALE * LOG2_E)
        bounded = (score_bound <= SCORE_BOUND_LIMIT).astype(jnp.int32).reshape(1)
        an_q, an_kv = q_a_norm[l][None], kv_a_norm[l][None]
        sgu_w_b = sgu_w[l].astype(BF16)
        sgu_b_b = jnp.broadcast_to(sgu_b[l][:, :, None], (SGU_G, SGU_CH, SGU_CH))
        conv_w, conv_b = ssd_conv_w[l], ssd_conv_b[l][None]
        bias, alog = _pad_lanes(ssd_dt_bias[l]), _pad_lanes(ssd_a_log[l])
        dvec = jnp.repeat(ssd_d[l], SSD_P)[None]
        fcw, fcb = ffn_conv_w[l], ffn_conv_b[l][None]

        def mixers(xs_in, mod, cs, t_rows, init_state, k_c=None, v_c=None, outputs=True):
            pq, pkv, psgu, pz, xbc, pdt = _in_proj(xs_in, mod, norm_mix[l][None], w_in_p, conv_w, conv_b, l, t_rows)
            k, v = _kv_proj(pkv, an_kv, w_ukv_b, l, gk1, gk2, cs, t_rows)
            yf, yb, fin = _ssd(xbc, pdt, bias, alog, tri, expand, init_state)
            if not outputs:
                return None, k, v, fin
            q = _q_proj(pq, an_q, w_uq_p, l, gq1, gq2, cs, t_rows)
            if k_c is None:
                att = _attention(bounded, q, k, v, tq=t_rows, tk=tk_att)
            else:
                att = _attention(bounded, q, k_c, v_c, k, v, tq=min(tq_att, q.shape[1]), tk=tk_att)
            sgu = _sgu(psgu, sgu_norm[l][None], sgu_w_b, sgu_b_b, t_rows)
            out = _out_proj(xs_in, mod, att, sgu, yf, yb, xbc, pz, attn_out_norm[l][None],
                            gmlp_out_norm[l][None], dvec, ssd_norm[l][None], norm_ffn[l][None], w_out_b, l, t_rows)
            return out, k, v, fin

        zero_state = jnp.zeros((batch, 2, SSD_G, SSD_N, GRP_W), F32)
        ctx_mid, k_c, v_c, ctx_state = mixers(x_ctx, mod_ctx, cs_ctx, tm_ctx, zero_state, outputs=need_ctx)
        (x_lat, h_lat), _, _, _ = mixers(x_lat, mod_lat, cs_lat, tm, ctx_state, k_c, v_c)
        x_lat = _ffn(x_lat, h_lat, mod_lat, wg, wu, fcw, fcb, wd, l, tm, tf)
        if need_ctx:
            x_ctx = _ffn(ctx_mid[0], ctx_mid[1], mod_ctx, wg, wu, fcw, fcb, wd, l, tm_ctx, tf)
    return x_lat
```

```python
import functools
import math

import jax
import jax.numpy as jnp
import numpy as np
from jax import lax
from jax.experimental import pallas as pl
from jax.experimental.pallas import tpu as pltpu

F32 = jnp.float32
BF16 = jnp.bfloat16
NORM_EPS = 1e-6

GRID_COLS = 64
N_HEADS = 8
NOPE = 128
ROPE = 64
QK = NOPE + ROPE
V_DIM = 128
Q_LORA = 768
KV_LORA = 512
ROPE_THETA = 10000.0
SM_SCALE = 1.0 / math.sqrt(QK)
LOG2_E = math.log2(math.e)
SCORE_BOUND_LIMIT = 60.0
HEAD_PAD = 256
ATT_W = N_HEADS * V_DIM
SGU_G = 4
SGU_CH = 128
SGU_W = 512
SSD_H = 8
SSD_P = 64
SSD_IN = SSD_H * SSD_P
SSD_N = 128
SSD_G = 2
SSD_Q = 128
XBC_W = SSD_IN + 2 * SSD_G * SSD_N
GRP_W = (SSD_H // SSD_G) * SSD_P

V7X_LANES = 128
V7X_SUBLANES = 8
BF16_ROWS = 2 * V7X_SUBLANES
V7X_VMEM_BYTES = 64 * 1024 * 1024
VMEM_LIMIT = V7X_VMEM_BYTES - 8 * 1024 * 1024

SEG_Q = (0, Q_LORA)
SEG_KV = (SEG_Q[1], SEG_Q[1] + KV_LORA + 2 * V7X_LANES)
SEG_SGU = (SEG_KV[1], SEG_KV[1] + 2 * SGU_W)
SEG_Z = (SEG_SGU[1], SEG_SGU[1] + SSD_IN)
SEG_XBC = (SEG_Z[1], SEG_Z[1] + XBC_W)
SEG_DT = (SEG_XBC[1], SEG_XBC[1] + V7X_LANES)
P_PAD = SEG_DT[1]
Q_HEAD_COLS = NOPE + 2 * V7X_LANES

_ROT_PERM = np.concatenate([np.arange(16, 32), np.arange(0, 16), np.arange(48, 64), np.arange(32, 48)])


def _params(sem, vmem=VMEM_LIMIT):
    return pltpu.CompilerParams(dimension_semantics=sem, vmem_limit_bytes=vmem)


def _layer_block(layer, *shape):
    zeros = (0,) * len(shape)
    return pl.BlockSpec((None,) + shape, lambda *_: (layer,) + zeros)


def _mod_block(layer, d, fixed_row):
    if fixed_row is None:
        return pl.BlockSpec((None, 1, V7X_SUBLANES, d), lambda bi, *_: (layer, bi, 0, 0))
    return pl.BlockSpec((None, 1, V7X_SUBLANES, d), lambda *_: (layer, fixed_row, 0, 0))


def _dot(a, b):
    return jnp.dot(a, b, preferred_element_type=F32)


def _dot_nt(a, b):
    return lax.dot_general(a, b, (((1,), (1,)), ((), ())), preferred_element_type=F32)


def _rms(t, gain):
    return t * lax.rsqrt(jnp.mean(t * t, axis=-1, keepdims=True) + NORM_EPS) * gain


def _silu(t):
    return t * (1.0 / (1.0 + jnp.exp(-t)))


def _gelu_tanh(t):
    return 0.5 * t * (1.0 + jnp.tanh(math.sqrt(2.0 / math.pi) * (t + 0.044715 * (t * t * t))))


def _softplus(t):
    return jnp.maximum(t, 0.0) + jnp.log1p(jnp.exp(-jnp.abs(t)))


def _split3(t):
    hi = t.astype(BF16)
    r1 = t - hi.astype(F32)
    mid = r1.astype(BF16)
    lo = (r1 - mid.astype(F32)).astype(BF16)
    return hi, mid, lo


def _mods_kernel(c_ref, w_ref, b_ref, o_ref):
    s = _silu(c_ref[...])
    s_hi = s.astype(BF16)
    s_lo = (s - s_hi.astype(F32)).astype(BF16)
    w = w_ref[0]
    w_hi = w.astype(BF16)
    w_lo = (w - w_hi.astype(F32)).astype(BF16)
    o_ref[0] = _dot(s_hi, w_hi) + _dot(s_lo, w_hi) + _dot(s_hi, w_lo) + b_ref[0]


def _mods(cc, w_mod, b_mod):
    depth, d, n6 = w_mod.shape
    tn = 1024
    return pl.pallas_call(
        _mods_kernel,
        out_shape=jax.ShapeDtypeStruct((depth, V7X_SUBLANES, n6), F32),
        grid=(depth, n6 // tn),
        in_specs=[pl.BlockSpec((V7X_SUBLANES, d), lambda l, j: (0, 0)),
                  pl.BlockSpec((1, d, tn), lambda l, j: (l, 0, j)),
                  pl.BlockSpec((1, 1, tn), lambda l, j: (l, 0, j))],
        out_specs=pl.BlockSpec((1, V7X_SUBLANES, tn), lambda l, j: (l, 0, j)),
        compiler_params=_params(("parallel", "parallel")),
        name="mods",
    )(cc, w_mod, b_mod.reshape(depth, 1, n6))


def _in_proj_kernel(x_ref, xp_ref, xn_ref, mod_ref, g_ref, w_ref, cw_ref, cb_ref,
                    q_ref, kv_ref, sgu_ref, z_ref, xbc_ref, dt_ref):
    i = pl.program_id(1)
    shift = mod_ref[0, 0:1, :]
    scale = mod_ref[0, 1:2, :]

    def norm_mod(t):
        return _rms(t, g_ref[...]) * (1.0 + scale) + shift

    a, b = SEG_XBC
    halo = jnp.concatenate([norm_mod(xp_ref[0]), norm_mod(xn_ref[0])], axis=0).astype(BF16)
    pre_halo = _dot(halo, w_ref[:, a:b])
    prev_row = jnp.where(i > 0, pre_halo[V7X_SUBLANES - 1:V7X_SUBLANES, :], 0.0)
    next_row = jnp.where(i < pl.num_programs(1) - 1, pre_halo[V7X_SUBLANES:V7X_SUBLANES + 1, :], 0.0)

    h = norm_mod(x_ref[0]).astype(BF16)
    tm = h.shape[0]
    pre = _dot(h, w_ref[:, a:b])
    row = lax.broadcasted_iota(jnp.int32, pre.shape, 0)
    up = jnp.where(row == 0, prev_row, pltpu.roll(pre, 1, axis=0))
    dn = jnp.where(row == tm - 1, next_row, pltpu.roll(pre, tm - 1, axis=0))
    conv = cb_ref[...] + cw_ref[0:1, :] * up + cw_ref[1:2, :] * pre + cw_ref[2:3, :] * dn
    xbc_ref[0] = _silu(conv).astype(xbc_ref.dtype)

    for ref, (a, b) in ((q_ref, SEG_Q), (kv_ref, SEG_KV), (sgu_ref, SEG_SGU), (z_ref, SEG_Z), (dt_ref, SEG_DT)):
        ref[0] = _dot(h, w_ref[:, a:b]).astype(ref.dtype)


def _in_proj(x, mods, mod_row, gain, w_in_p, conv_w, conv_b, layer, tm):
    b, s, d = x.shape
    segs = (SEG_Q, SEG_KV, SEG_SGU, SEG_Z, SEG_XBC, SEG_DT)
    dts = (BF16, BF16, BF16, BF16, BF16, F32)
    per = tm // V7X_SUBLANES
    last_blk = s // V7X_SUBLANES - 1
    return pl.pallas_call(
        _in_proj_kernel,
        out_shape=[jax.ShapeDtypeStruct((b, s, hi - lo), dt) for (lo, hi), dt in zip(segs, dts)],
        grid=(b, s // tm),
        in_specs=[pl.BlockSpec((1, tm, d), lambda bi, i: (bi, i, 0)),
                  pl.BlockSpec((1, V7X_SUBLANES, d), lambda bi, i: (bi, jnp.maximum(i * per - 1, 0), 0)),
                  pl.BlockSpec((1, V7X_SUBLANES, d), lambda bi, i: (bi, jnp.minimum((i + 1) * per, last_blk), 0)),
                  _mod_block(layer, d, mod_row),
                  _layer_block(layer, 1, d),
                  pl.BlockSpec((None, d, P_PAD), lambda bi, i: (layer, 0, 0), pipeline_mode=pl.Buffered(1)),
                  _layer_block(layer, 3, XBC_W),
                  _layer_block(layer, 1, XBC_W)],
        out_specs=[pl.BlockSpec((1, tm, hi - lo), lambda bi, i: (bi, i, 0)) for lo, hi in segs],
        compiler_params=_params(("parallel", "parallel")),
        name="in_proj",
    )(x, x, x, mods, gain, w_in_p, conv_w, conv_b)


def _q_proj_kernel(cq_ref, an_ref, w_ref, g_ref, g2_ref, cs_ref, q_ref):
    cq = cq_ref[0].astype(F32)
    cqn = _rms(cq, an_ref[...]).astype(BF16)
    gcs = g2_ref[...] * cs_ref[...]
    gc, gs = gcs[:, :V7X_LANES], gcs[:, V7X_LANES:]
    lanes = V7X_LANES
    for pair in range(N_HEADS // 2):
        y = _dot(cqn, w_ref[:, pair * 2 * Q_HEAD_COLS:(pair + 1) * 2 * Q_HEAD_COLS])
        for e in range(2):
            h = 2 * pair + e
            a = y[:, e * lanes:(e + 1) * lanes]
            r = y[:, (2 + e) * lanes:(3 + e) * lanes]
            rp = y[:, (4 + e) * lanes:(5 + e) * lanes]
            ssq = jnp.sum(a * a + r * r, axis=-1, keepdims=True)
            rs = lax.rsqrt(ssq * (1.0 / QK) + NORM_EPS) * (SM_SCALE * LOG2_E)
            q_ref[0, :, h * HEAD_PAD:h * HEAD_PAD + NOPE] = (a * g_ref[...] * rs).astype(BF16)
            q_ref[0, :, h * HEAD_PAD + NOPE:(h + 1) * HEAD_PAD] = ((r * gc + rp * gs) * rs).astype(BF16)


def _q_proj(cq, an, w_uq_p, layer, g, g2, cs, tm):
    b, s, _ = cq.shape
    width = N_HEADS * HEAD_PAD
    return pl.pallas_call(
        _q_proj_kernel,
        out_shape=jax.ShapeDtypeStruct((b, s, width), BF16),
        grid=(b, s // tm),
        in_specs=[pl.BlockSpec((1, tm, Q_LORA), lambda bi, i: (bi, i, 0)),
                  _layer_block(layer, 1, Q_LORA),
                  _layer_block(layer, Q_LORA, N_HEADS * Q_HEAD_COLS),
                  _layer_block(layer, 1, NOPE),
                  _layer_block(layer, 1, 2 * V7X_LANES),
                  pl.BlockSpec((tm, 2 * V7X_LANES), lambda bi, i: (i, 0))],
        out_specs=pl.BlockSpec((1, tm, width), lambda bi, i: (bi, i, 0)),
        compiler_params=_params(("parallel", "parallel")),
        name="q_proj",
    )(cq, an, w_uq_p, g, g2, cs)


def _kv_proj_kernel(t_ref, an_ref, w_ref, g_ref, g2_ref, cs_ref, k_ref, v_ref):
    ckv = t_ref[0, :, :KV_LORA].astype(F32)
    kr = t_ref[0, :, KV_LORA:KV_LORA + V7X_LANES].astype(F32)
    krp = t_ref[0, :, KV_LORA + V7X_LANES:].astype(F32)
    ckvn = _rms(ckv, an_ref[...]).astype(BF16)
    gcs = g2_ref[...] * cs_ref[...]
    rot = kr * gcs[:, :V7X_LANES] + krp * gcs[:, V7X_LANES:]
    kr_sq = kr * kr
    lane = lax.broadcasted_iota(jnp.int32, kr.shape, 1)
    ones_col = jnp.where(lane == 0, 1.0, 0.0).astype(BF16)
    for h in range(N_HEADS):
        y = _dot(ckvn, w_ref[:, h * 2 * NOPE:(h + 1) * 2 * NOPE])
        kn = y[:, :NOPE]
        rs = lax.rsqrt(jnp.sum(kn * kn + kr_sq, axis=-1, keepdims=True) * (1.0 / QK) + NORM_EPS)
        k_ref[0, :, h * HEAD_PAD:h * HEAD_PAD + NOPE] = (kn * g_ref[...] * rs).astype(BF16)
        k_ref[0, :, h * HEAD_PAD + NOPE:(h + 1) * HEAD_PAD] = (rot * rs).astype(BF16)
        v_ref[0, :, h * HEAD_PAD:h * HEAD_PAD + V_DIM] = y[:, NOPE:].astype(BF16)
        v_ref[0, :, h * HEAD_PAD + V_DIM:(h + 1) * HEAD_PAD] = ones_col


def _kv_proj(t, an, w_ukv, layer, g, g2, cs, tm):
    b, s, wt = t.shape
    return pl.pallas_call(
        _kv_proj_kernel,
        out_shape=[jax.ShapeDtypeStruct((b, s, N_HEADS * HEAD_PAD), BF16),
                   jax.ShapeDtypeStruct((b, s, N_HEADS * HEAD_PAD), BF16)],
        grid=(b, s // tm),
        in_specs=[pl.BlockSpec((1, tm, wt), lambda bi, i: (bi, i, 0)),
                  _layer_block(layer, 1, KV_LORA),
                  _layer_block(layer, KV_LORA, N_HEADS * 2 * NOPE),
                  _layer_block(layer, 1, NOPE),
                  _layer_block(layer, 1, 2 * V7X_LANES),
                  pl.BlockSpec((tm, 2 * V7X_LANES), lambda bi, i: (i, 0))],
        out_specs=[pl.BlockSpec((1, tm, N_HEADS * HEAD_PAD), lambda bi, i: (bi, i, 0)),
                   pl.BlockSpec((1, tm, N_HEADS * HEAD_PAD), lambda bi, i: (bi, i, 0))],
        compiler_params=_params(("parallel", "parallel")),
        name="kv_proj",
    )(t, an, w_ukv, g, g2, cs)


def _attn_kernel(*refs, layer, n_lat, tk):
    if n_lat:
        bounded_ref, q_ref, kc_ref, vc_ref, kl_ref, vl_ref, o_ref, s0_ref, s1_ref = refs
    else:
        bounded_ref, q_ref, kc_ref, vc_ref, o_ref = refs
    n_chunks = n_lat // tk if n_lat else 0

    def finish(acc):
        o_ref[0] = (acc[:, :V_DIM] * (1.0 / acc[:, V_DIM:V_DIM + 1])).astype(o_ref.dtype)

    @pl.when(bounded_ref[layer] != 0)
    def _():
        q = q_ref[0]
        acc = _dot(jnp.exp2(_dot_nt(q, kc_ref[0])).astype(BF16), vc_ref[0])
        for j in range(n_chunks):
            p = jnp.exp2(_dot_nt(q, kl_ref[0, j * tk:(j + 1) * tk, :])).astype(BF16)
            acc = acc + _dot(p, vl_ref[0, j * tk:(j + 1) * tk, :])
        finish(acc)

    @pl.when(bounded_ref[layer] == 0)
    def _():
        q = q_ref[0]
        tq = q.shape[0]

        def update(carry, s, v):
            m, acc = carry
            m_new = jnp.maximum(m, jnp.max(s, axis=-1, keepdims=True))
            p = jnp.exp2(s - m_new).astype(BF16)
            return m_new, jnp.exp2(m - m_new) * acc + _dot(p, v)

        carry = (jnp.full((tq, 1), -jnp.inf, F32), jnp.zeros((tq, HEAD_PAD), F32))
        if n_lat:
            slots = (s0_ref, s1_ref)

            def scores(j):
                slots[j % 2][...] = _dot_nt(q, kl_ref[0, j * tk:(j + 1) * tk, :])

            scores(0)
            carry = update(carry, _dot_nt(q, kc_ref[0]), vc_ref[0])
            for j in range(n_chunks):
                if j + 1 < n_chunks:
                    scores(j + 1)
                carry = update(carry, slots[j % 2][...], vl_ref[0, j * tk:(j + 1) * tk, :])
        else:
            carry = update(carry, _dot_nt(q, kc_ref[0]), vc_ref[0])
        finish(carry[1])


def _attention(bounded, layer, q, k_c, v_c, k_l=None, v_l=None, *, tq, tk):
    b, s, _ = q.shape
    n_ctx = k_c.shape[1]
    n_lat = 0 if k_l is None else k_l.shape[1]
    in_specs = [pl.BlockSpec(memory_space=pltpu.SMEM),
                pl.BlockSpec((1, tq, HEAD_PAD), lambda bi, h, i: (bi, i, h)),
                pl.BlockSpec((1, n_ctx, HEAD_PAD), lambda bi, h, i: (bi, 0, h)),
                pl.BlockSpec((1, n_ctx, HEAD_PAD), lambda bi, h, i: (bi, 0, h))]
    args = [bounded, q, k_c, v_c]
    scratch = []
    if n_lat:
        tk = min(tk, n_lat)
        assert n_lat % tk == 0
        in_specs += [pl.BlockSpec((1, n_lat, HEAD_PAD), lambda bi, h, i: (bi, 0, h)),
                     pl.BlockSpec((1, n_lat, HEAD_PAD), lambda bi, h, i: (bi, 0, h))]
        args += [k_l, v_l]
        scratch = [pltpu.VMEM((tq, tk), F32), pltpu.VMEM((tq, tk), F32)]
    return pl.pallas_call(
        functools.partial(_attn_kernel, layer=layer, n_lat=n_lat, tk=tk),
        out_shape=jax.ShapeDtypeStruct((b, s, ATT_W), BF16),
        grid=(b, N_HEADS, s // tq),
        in_specs=in_specs,
        out_specs=pl.BlockSpec((1, tq, V_DIM), lambda bi, h, i: (bi, i, h)),
        scratch_shapes=scratch,
        compiler_params=_params(("parallel", "parallel", "arbitrary")),
        name="attention",
    )(*args)


def _sgu_kernel(p_ref, gn_ref, w_ref, b_ref, o_ref, *, n_chunks):
    for c in range(n_chunks):
        rows = slice(c * SGU_CH, (c + 1) * SGU_CH)
        z = _gelu_tanh(p_ref[0, rows, :].astype(F32))
        u = z[:, :SGU_W]
        vn = _rms(z[:, SGU_W:], gn_ref[...]).astype(BF16)
        for g in range(SGU_G):
            cols = slice(g * SGU_CH, (g + 1) * SGU_CH)
            mixed = _dot(w_ref[g], vn[:, cols]) + b_ref[g]
            o_ref[0, rows, cols] = (u[:, cols] * mixed).astype(o_ref.dtype)


def _sgu(p, gn, w_s, b_b, layer, rows):
    b, s, _ = p.shape
    return pl.pallas_call(
        functools.partial(_sgu_kernel, n_chunks=rows // SGU_CH),
        out_shape=jax.ShapeDtypeStruct((b, s, SGU_W), BF16),
        grid=(b, s // rows),
        in_specs=[pl.BlockSpec((1, rows, 2 * SGU_W), lambda bi, i: (bi, i, 0)),
                  _layer_block(layer, 1, SGU_W),
                  _layer_block(layer, SGU_G, SGU_CH, SGU_CH),
                  _layer_block(layer, SGU_G, SGU_CH, SGU_CH)],
        out_specs=pl.BlockSpec((1, rows, SGU_W), lambda bi, i: (bi, i, 0)),
        compiler_params=_params(("parallel", "parallel")),
        name="sgu",
    )(p, gn, w_s, b_b)


def _ssd_direction(x_ref, dt_ref, bias_ref, alog_ref, tri_ref, exp_ref, state_ref, y_ref, *, reverse):
    q = SSD_Q
    xs = x_ref[0, :, :SSD_IN].astype(F32)
    dt = _softplus(dt_ref[0] + bias_ref[...])
    a_dt = dt * (-jnp.exp(alog_ref[...]))
    tri = tri_ref[...]
    acs = sum(_dot(tri, part) for part in _split3(a_dt))
    acs_t = acs.T
    expand = exp_ref[...]
    acs_x = sum(_dot(part, expand) for part in _split3(acs))
    dt_hi = dt.astype(BF16)
    dt_x = _dot(dt_hi, expand) + _dot((dt - dt_hi.astype(F32)).astype(BF16), expand)
    last = acs_x[0:1, :] if reverse else acs_x[q - 1:q, :]
    xdt = xs * dt_x
    x_end = (xdt * jnp.exp(last - acs_x)).astype(BF16)
    decay_out = jnp.exp(acs_x)
    chunk_decay = jnp.exp(last)

    ti = lax.broadcasted_iota(jnp.int32, (q, q), 0)
    tj = lax.broadcasted_iota(jnp.int32, (q, q), 1)
    keep = (ti <= tj) if reverse else (ti >= tj)
    lane_blk = lax.shift_right_logical(lax.broadcasted_iota(jnp.int32, (q, GRP_W), 1), int(math.log2(SSD_P)))
    head0 = SSD_H if reverse else 0
    heads_per_group = SSD_H // SSD_G
    for g in range(SSD_G):
        bm = x_ref[0, :, SSD_IN + g * SSD_N:SSD_IN + (g + 1) * SSD_N]
        cm = x_ref[0, :, SSD_IN + (SSD_G + g) * SSD_N:SSD_IN + (SSD_G + g + 1) * SSD_N]
        cb = _dot_nt(cm, bm)
        cols = slice(g * GRP_W, (g + 1) * GRP_W)
        state = state_ref[g]
        y_off = _dot(cm, state.astype(BF16)) * decay_out[:, cols]
        state_ref[g] = state * chunk_decay[:, cols] + _dot(bm.astype(F32).T.astype(BF16), x_end[:, cols])
        blocks, stacked = [], []
        for hh in range(heads_per_group):
            hcol = head0 + g * heads_per_group + hh
            seg = acs[:, hcol:hcol + 1] - acs_t[hcol:hcol + 1, :]
            blocks.append((cb * jnp.exp(jnp.where(keep, seg, -jnp.inf))).astype(BF16))
            stacked.append(jnp.where(lane_blk == hh, xdt[:, cols], 0.0).astype(BF16))
        y = y_off + _dot(jnp.concatenate(blocks, axis=1), jnp.concatenate(stacked, axis=0))
        y_ref[0, :, cols] = y.astype(y_ref.dtype)


def _ssd_kernel(xf_ref, dtf_ref, xb_ref, dtb_ref, bias_ref, alog_ref, tri_ref, exp_ref, init_ref,
                yf_ref, yb_ref, fin_ref, state_ref):
    s = pl.program_id(1)

    @pl.when(s == 0)
    def _():
        state_ref[...] = init_ref[0]

    _ssd_direction(xf_ref, dtf_ref, bias_ref, alog_ref, tri_ref.at[0], exp_ref.at[0], state_ref.at[0], yf_ref,
                   reverse=False)
    _ssd_direction(xb_ref, dtb_ref, bias_ref, alog_ref, tri_ref.at[1], exp_ref.at[1], state_ref.at[1], yb_ref,
                   reverse=True)

    @pl.when(s == pl.num_programs(1) - 1)
    def _():
        fin_ref[0] = state_ref[...]


def _ssd(xbc, dt, bias, alog, tri, expand, init, layer):
    b, s, _ = xbc.shape
    nc = s // SSD_Q

    def main_f(bi, i): return (bi, i, 0)
    def main_b(bi, i): return (bi, nc - 1 - i, 0)
    const2 = lambda bi, i: (0, 0)
    const3 = lambda bi, i: (0, 0, 0)
    state_shape = (2, SSD_G, SSD_N, GRP_W)
    return pl.pallas_call(
        _ssd_kernel,
        out_shape=[jax.ShapeDtypeStruct((b, s, SSD_IN), BF16), jax.ShapeDtypeStruct((b, s, SSD_IN), BF16),
                   jax.ShapeDtypeStruct((b,) + state_shape, F32)],
        grid=(b, nc),
        in_specs=[pl.BlockSpec((1, SSD_Q, XBC_W), main_f), pl.BlockSpec((1, SSD_Q, V7X_LANES), main_f),
                  pl.BlockSpec((1, SSD_Q, XBC_W), main_b), pl.BlockSpec((1, SSD_Q, V7X_LANES), main_b),
                  _layer_block(layer, 1, V7X_LANES), _layer_block(layer, 1, V7X_LANES),
                  pl.BlockSpec((2, SSD_Q, SSD_Q), const3), pl.BlockSpec((2, V7X_LANES, SSD_IN), const3),
                  pl.BlockSpec((1,) + state_shape, lambda bi, i: (bi, 0, 0, 0, 0))],
        out_specs=[pl.BlockSpec((1, SSD_Q, SSD_IN), main_f), pl.BlockSpec((1, SSD_Q, SSD_IN), main_b),
                   pl.BlockSpec((1,) + state_shape, lambda bi, i: (bi, 0, 0, 0, 0))],
        scratch_shapes=[pltpu.VMEM(state_shape, F32)],
        compiler_params=_params(("parallel", "arbitrary")),
        name="ssd",
    )(xbc, dt, xbc, dt, bias, alog, tri, expand, init)


def _out_proj_kernel(x_ref, mod_ref, att_ref, sgu_ref, yf_ref, yb_ref, xs_ref, z_ref,
                     ga_ref, gs_ref, d_ref, gy_ref, gf_ref, w_ref, o_ref, h_ref):
    att = _rms(att_ref[0].astype(F32), ga_ref[...]).astype(BF16)
    sgu = _rms(sgu_ref[0].astype(F32), gs_ref[...]).astype(BF16)
    y = yf_ref[0].astype(F32) + yb_ref[0].astype(F32) + d_ref[...] * xs_ref[0].astype(F32)
    ssd = _rms(y * _silu(z_ref[0].astype(F32)), gy_ref[...]).astype(BF16)
    mix = (_dot(att, w_ref[:ATT_W, :]) + _dot(sgu, w_ref[ATT_W:ATT_W + SGU_W, :])
           + _dot(ssd, w_ref[ATT_W + SGU_W:, :]))
    x_new = x_ref[0] + mod_ref[0, 2:3, :] * mix
    o_ref[0] = x_new
    h_ref[0] = (_rms(x_new, gf_ref[...]) * (1.0 + mod_ref[0, 4:5, :]) + mod_ref[0, 3:4, :]).astype(BF16)


def _out_proj(x, mods, mod_row, att, sgu, yf, yb, xs, z, ga, gs, dvec, gy, gf, w_out, layer, tm):
    b, s, d = x.shape
    row = lambda w: pl.BlockSpec((1, tm, w), lambda bi, i: (bi, i, 0))
    vec = lambda w: _layer_block(layer, 1, w)
    return pl.pallas_call(
        _out_proj_kernel,
        out_shape=[jax.ShapeDtypeStruct((b, s, d), F32), jax.ShapeDtypeStruct((b, s, d), BF16)],
        grid=(b, s // tm),
        in_specs=[row(d), _mod_block(layer, d, mod_row),
                  row(ATT_W), row(SGU_W), row(SSD_IN), row(SSD_IN), row(SSD_IN), row(SSD_IN),
                  vec(ATT_W), vec(SGU_W), vec(SSD_IN), vec(SSD_IN), vec(d),
                  pl.BlockSpec((None,) + w_out.shape[1:], lambda bi, i: (layer, 0, 0),
                               pipeline_mode=pl.Buffered(1))],
        out_specs=[row(d), row(d)],
        compiler_params=_params(("parallel", "parallel")),
        name="out_proj",
    )(x, mods, att, sgu, yf, yb, xs, z, ga, gs, dvec, gy, gf, w_out)


def _ffn_kernel(x_ref, h_ref, hp_ref, hn_ref, mod_ref, wg_ref, wu_ref, cw_ref, cb_ref, wd_ref, o_ref,
                halo_sc, acc_sc):
    i = pl.program_id(1)
    j = pl.program_id(2)
    hb = BF16_ROWS

    @pl.when(j == 0)
    def _():
        halo_sc[0:hb, :] = jnp.where(i > 0, hp_ref[0], jnp.zeros_like(hp_ref[0]))
        halo_sc[hb:, :] = jnp.where(i < pl.num_programs(1) - 1, hn_ref[0], jnp.zeros_like(hn_ref[0]))
        acc_sc[...] = jnp.zeros_like(acc_sc)

    h = h_ref[0]
    tm = h.shape[0]
    gate = _dot(h, wg_ref[...])
    gate_halo = _dot(halo_sc[...], wg_ref[...])
    row = lax.broadcasted_iota(jnp.int32, gate.shape, 0)
    up = jnp.where(row == 0, gate_halo[hb - 1:hb, :], pltpu.roll(gate, 1, axis=0))
    dn = jnp.where(row == tm - 1, gate_halo[hb:hb + 1, :], pltpu.roll(gate, tm - 1, axis=0))
    conv = cb_ref[...] + cw_ref[0:1, :] * up + cw_ref[1:2, :] * gate + cw_ref[2:3, :] * dn
    act = (_silu(conv) * _dot(h, wu_ref[...])).astype(BF16)
    acc_sc[...] += _dot(act, wd_ref[...])

    @pl.when(j == pl.num_programs(2) - 1)
    def _():
        o_ref[0] = x_ref[0] + mod_ref[0, 5:6, :] * acc_sc[...]


def _ffn(x, h, mods, mod_row, w_gate, w_up, conv_w, conv_b, w_down, layer, tm, tf):
    b, s, d = x.shape
    f = w_down.shape[1]
    per = tm // BF16_ROWS
    last_blk = s // BF16_ROWS - 1
    assert f % tf == 0
    return pl.pallas_call(
        _ffn_kernel,
        out_shape=jax.ShapeDtypeStruct((b, s, d), F32),
        grid=(b, s // tm, f // tf),
        in_specs=[pl.BlockSpec((1, tm, d), lambda bi, i, j: (bi, i, 0)),
                  pl.BlockSpec((1, tm, d), lambda bi, i, j: (bi, i, 0)),
                  pl.BlockSpec((1, BF16_ROWS, d), lambda bi, i, j: (bi, jnp.maximum(i * per - 1, 0), 0)),
                  pl.BlockSpec((1, BF16_ROWS, d), lambda bi, i, j: (bi, jnp.minimum((i + 1) * per, last_blk), 0)),
                  _mod_block(layer, d, mod_row),
                  pl.BlockSpec((None, d, tf), lambda bi, i, j: (layer, 0, j)),
                  pl.BlockSpec((None, d, tf), lambda bi, i, j: (layer, 0, j)),
                  pl.BlockSpec((None, 3, tf), lambda bi, i, j: (layer, 0, j)),
                  pl.BlockSpec((None, 1, tf), lambda bi, i, j: (layer, 0, j)),
                  pl.BlockSpec((None, tf, d), lambda bi, i, j: (layer, j, 0))],
        out_specs=pl.BlockSpec((1, tm, d), lambda bi, i, j: (bi, i, 0)),
        scratch_shapes=[pltpu.VMEM((2 * BF16_ROWS, d), BF16), pltpu.VMEM((tm, d), F32)],
        compiler_params=_params(("parallel", "parallel", "arbitrary")),
        name="ffn",
    )(x, h, h, h, mods, w_gate, w_up, conv_w, conv_b, w_down)


def _rope_tables(seq):
    rows = seq // GRID_COLS
    pairs = ROPE // 4
    freqs = ROPE_THETA ** (-jnp.arange(pairs, dtype=F32) / pairs)
    ar = jnp.arange(rows, dtype=F32)[:, None] * freqs
    ac = jnp.arange(GRID_COLS, dtype=F32)[:, None] * freqs
    by_row = lambda t: jnp.repeat(t, GRID_COLS, axis=0)
    by_col = lambda t: jnp.tile(t, (rows, 1))
    cr, sr, cc, sc = by_row(jnp.cos(ar)), by_row(jnp.sin(ar)), by_col(jnp.cos(ac)), by_col(jnp.sin(ac))
    cos = jnp.concatenate([cr, cr, cc, cc], axis=-1)
    sin = jnp.concatenate([-sr, sr, -sc, sc], axis=-1)
    zeros = jnp.zeros_like(cos)
    return jnp.concatenate([cos, zeros, sin, zeros], axis=-1)


def _pack_w_in(w):
    w = w.astype(BF16)
    off_kr, off_sgu = Q_LORA + KV_LORA, Q_LORA + KV_LORA + ROPE
    off_dt = off_sgu + 2 * SGU_W + SSD_IN + XBC_W
    kr = w[..., off_kr:off_sgu]
    dt = w[..., off_dt:]
    z64 = jnp.zeros(w.shape[:2] + (V7X_LANES - ROPE,), BF16)
    return jnp.concatenate([w[..., :off_kr], kr, z64, kr[..., _ROT_PERM], z64, w[..., off_sgu:off_dt], dt,
                            jnp.zeros(w.shape[:2] + (V7X_LANES - dt.shape[-1],), BF16)], axis=-1)


def _pack_w_uq(w):
    depth = w.shape[0]
    pairs = N_HEADS // 2
    w = w.astype(BF16).reshape(depth, Q_LORA, pairs, 2, QK)
    rot = w[..., NOPE:]
    z64 = jnp.zeros((depth, Q_LORA, pairs, 2, V7X_LANES - ROPE), BF16)
    both = lambda t: t.reshape(depth, Q_LORA, pairs, 2 * V7X_LANES)
    return jnp.concatenate([both(w[..., :NOPE]), both(jnp.concatenate([rot, z64], axis=-1)),
                            both(jnp.concatenate([rot[..., _ROT_PERM], z64], axis=-1))],
                           axis=-1).reshape(depth, Q_LORA, N_HEADS * Q_HEAD_COLS)


def _rot_gains(g):
    rot = g[:, NOPE:]
    z64 = jnp.zeros((g.shape[0], V7X_LANES - ROPE), g.dtype)
    return jnp.concatenate([rot, z64, rot[:, _ROT_PERM], z64], axis=-1)[:, None, :]


def _row_vectors(v, width=None):
    v = v.reshape(v.shape[0], 1, -1)
    return v if width is None else jnp.pad(v, ((0, 0), (0, 0), (0, width - v.shape[-1])))


def kernel(x, c, ctx, c_ctx, w_mod, b_mod, norm_mix, norm_ffn, w_in, q_a_norm, w_uq, kv_a_norm, w_ukv, q_norm, k_norm, attn_out_norm, sgu_norm, sgu_w, sgu_b, gmlp_out_norm, ssd_conv_w, ssd_conv_b, ssd_dt_bias, ssd_a_log, ssd_d, ssd_norm, w_out, ffn_w_gate, ffn_w_up, ffn_conv_w, ffn_conv_b, ffn_w_down):
    batch, seq, d = x.shape
    n_ctx = ctx.shape[1]
    depth = w_mod.shape[0]
    tm = min(512, seq)
    tm_ctx = min(256, n_ctx)
    tf = 512
    tk_att = 1024
    tq_att = 1024

    cc = jnp.concatenate([c, c_ctx[None], jnp.zeros((V7X_SUBLANES - batch - 1, d), F32)], axis=0)
    mods = jnp.pad(_mods(cc, w_mod, b_mod).reshape(depth, V7X_SUBLANES, 6, d),
                   ((0, 0), (0, 0), (0, V7X_SUBLANES - 6), (0, 0)))
    ctx_row = batch

    cs_lat = _rope_tables(seq)
    cs_ctx = jnp.concatenate([jnp.ones((n_ctx, ROPE), F32), jnp.zeros((n_ctx, 2 * V7X_LANES - ROPE), F32)], axis=-1)

    ti = np.arange(SSD_Q)
    tri = jnp.asarray(np.stack([ti[:, None] >= ti[None, :], ti[:, None] <= ti[None, :]]), BF16)
    lane_head = np.arange(SSD_IN) // SSD_P
    col = np.arange(V7X_LANES)
    expand = jnp.asarray(np.stack([col[:, None] == lane_head[None, :],
                                   col[:, None] == lane_head[None, :] + SSD_H]), BF16)

    w_in_p, w_uq_p = _pack_w_in(w_in), _pack_w_uq(w_uq)
    w_ukv_b, w_out_b = w_ukv.astype(BF16), w_out.astype(BF16)
    wg, wu, wd = ffn_w_gate.astype(BF16), ffn_w_up.astype(BF16), ffn_w_down.astype(BF16)

    g_mix, g_ffn = _row_vectors(norm_mix), _row_vectors(norm_ffn)
    an_q, an_kv = _row_vectors(q_a_norm), _row_vectors(kv_a_norm)
    gq1, gk1 = _row_vectors(q_norm[:, :NOPE]), _row_vectors(k_norm[:, :NOPE])
    gq2, gk2 = _rot_gains(q_norm), _rot_gains(k_norm)
    g_att, g_sgu_in, g_sgu_out = _row_vectors(attn_out_norm), _row_vectors(sgu_norm), _row_vectors(gmlp_out_norm)
    g_ssd = _row_vectors(ssd_norm)
    sgu_w_b = sgu_w.astype(BF16)
    sgu_b_b = jnp.broadcast_to(sgu_b[..., None], sgu_b.shape + (SGU_CH,))
    conv_b = _row_vectors(ssd_conv_b)
    bias, alog = _row_vectors(ssd_dt_bias, V7X_LANES), _row_vectors(ssd_a_log, V7X_LANES)
    dvec = _row_vectors(jnp.repeat(ssd_d, SSD_P, axis=1))
    fcb = _row_vectors(ffn_conv_b)
    score_bound = (QK * SM_SCALE * LOG2_E) * jnp.max(jnp.abs(q_norm), axis=1) * jnp.max(jnp.abs(k_norm), axis=1)
    bounded = (score_bound <= SCORE_BOUND_LIMIT).astype(jnp.int32)

    x_lat, x_ctx = x, ctx
    for l in range(depth):
        need_ctx = l < depth - 1

        def mixers(xs_in, mod_row, cs, t_rows, init_state, k_c=None, v_c=None, outputs=True):
            pq, pkv, psgu, pz, xbc, pdt = _in_proj(xs_in, mods, mod_row, g_mix, w_in_p, ssd_conv_w, conv_b, l, t_rows)
            k, v = _kv_proj(pkv, an_kv, w_ukv_b, l, gk1, gk2, cs, t_rows)
            yf, yb, fin = _ssd(xbc, pdt, bias, alog, tri, expand, init_state, l)
            if not outputs:
                return None, k, v, fin
            q = _q_proj(pq, an_q, w_uq_p, l, gq1, gq2, cs, t_rows)
            if k_c is None:
                att = _attention(bounded, l, q, k, v, tq=t_rows, tk=tk_att)
            else:
                att = _attention(bounded, l, q, k_c, v_c, k, v, tq=min(tq_att, q.shape[1]), tk=tk_att)
            sgu = _sgu(psgu, g_sgu_in, sgu_w_b, sgu_b_b, l, t_rows)
            out = _out_proj(xs_in, mods, mod_row, att, sgu, yf, yb, xbc, pz, g_att, g_sgu_out, dvec, g_ssd, g_ffn,
                            w_out_b, l, t_rows)
            return out, k, v, fin

        zero_state = jnp.zeros((batch, 2, SSD_G, SSD_N, GRP_W), F32)
        ctx_mid, k_c, v_c, ctx_state = mixers(x_ctx, ctx_row, cs_ctx, tm_ctx, zero_state, outputs=need_ctx)
        (x_lat, h_lat), _, _, _ = mixers(x_lat, None, cs_lat, tm, ctx_state, k_c, v_c)
        x_lat = _ffn(x_lat, h_lat, mods, None, wg, wu, ffn_conv_w, fcb, wd, l, tm, tf)
        if need_ctx:
            x_ctx = _ffn(ctx_mid[0], ctx_mid[1], mods, ctx_row, wg, wu, ffn_conv_w, fcb, wd, l, tm_ctx, tf)
    return x_lat
```

```python
import functools
import math

import jax
import jax.numpy as jnp
import numpy as np
from jax import lax
from jax.experimental import pallas as pl
from jax.experimental.pallas import tpu as pltpu

F32 = jnp.float32
BF16 = jnp.bfloat16
NORM_EPS = 1e-6

GRID_COLS = 64
N_HEADS = 8
NOPE = 128
ROPE = 64
QK = NOPE + ROPE
V_DIM = 128
Q_LORA = 768
KV_LORA = 512
ROPE_THETA = 10000.0
SM_SCALE = 1.0 / math.sqrt(QK)
LOG2_E = math.log2(math.e)
SCORE_BOUND_LIMIT = 60.0
HEAD_PAD = 256
ATT_W = N_HEADS * V_DIM
SGU_G = 4
SGU_CH = 128
SGU_W = 512
SSD_H = 8
SSD_P = 64
SSD_IN = SSD_H * SSD_P
SSD_N = 128
SSD_G = 2
SSD_Q = 128
XBC_W = SSD_IN + 2 * SSD_G * SSD_N
GRP_W = (SSD_H // SSD_G) * SSD_P

V7X_LANES = 128
V7X_SUBLANES = 8
BF16_ROWS = 2 * V7X_SUBLANES
V7X_VMEM_BYTES = 64 * 1024 * 1024
VMEM_LIMIT = V7X_VMEM_BYTES - 8 * 1024 * 1024

SEG_Q = (0, Q_LORA)
SEG_KV = (SEG_Q[1], SEG_Q[1] + KV_LORA + 2 * V7X_LANES)
SEG_SGU = (SEG_KV[1], SEG_KV[1] + 2 * SGU_W)
SEG_Z = (SEG_SGU[1], SEG_SGU[1] + SSD_IN)
SEG_XBC = (SEG_Z[1], SEG_Z[1] + XBC_W)
SEG_DT = (SEG_XBC[1], SEG_XBC[1] + V7X_LANES)
P_PAD = SEG_DT[1]
Q_HEAD_COLS = NOPE + 2 * V7X_LANES

_ROT_PERM = np.concatenate([np.arange(16, 32), np.arange(0, 16), np.arange(48, 64), np.arange(32, 48)])


def _params(sem, vmem=VMEM_LIMIT):
    return pltpu.CompilerParams(dimension_semantics=sem, vmem_limit_bytes=vmem)


def _layer_block(layer, *shape):
    zeros = (0,) * len(shape)
    return pl.BlockSpec((None,) + shape, lambda *_: (layer,) + zeros)


def _mod_block(layer, d, fixed_row):
    if fixed_row is None:
        return pl.BlockSpec((None, 1, V7X_SUBLANES, d), lambda bi, *_: (layer, bi, 0, 0))
    return pl.BlockSpec((None, 1, V7X_SUBLANES, d), lambda *_: (layer, fixed_row, 0, 0))


def _dot(a, b):
    return jnp.dot(a, b, preferred_element_type=F32)


def _dot_nt(a, b):
    return lax.dot_general(a, b, (((1,), (1,)), ((), ())), preferred_element_type=F32)


def _rms(t, gain):
    return t * lax.rsqrt(jnp.mean(t * t, axis=-1, keepdims=True) + NORM_EPS) * gain


def _silu(t):
    return t * (1.0 / (1.0 + jnp.exp(-t)))


def _gelu_tanh(t):
    return 0.5 * t * (1.0 + jnp.tanh(math.sqrt(2.0 / math.pi) * (t + 0.044715 * (t * t * t))))


def _softplus(t):
    return jnp.maximum(t, 0.0) + jnp.log1p(jnp.exp(-jnp.abs(t)))


def _split3(t):
    hi = t.astype(BF16)
    r1 = t - hi.astype(F32)
    mid = r1.astype(BF16)
    lo = (r1 - mid.astype(F32)).astype(BF16)
    return hi, mid, lo


def _mods_kernel(c_ref, w_ref, b_ref, o_ref):
    s = _silu(c_ref[...])
    s_hi = s.astype(BF16)
    s_lo = (s - s_hi.astype(F32)).astype(BF16)
    w = w_ref[0]
    w_hi = w.astype(BF16)
    w_lo = (w - w_hi.astype(F32)).astype(BF16)
    o_ref[0] = _dot(s_hi, w_hi) + _dot(s_lo, w_hi) + _dot(s_hi, w_lo) + b_ref[0]


def _mods(cc, w_mod, b_mod):
    depth, d, n6 = w_mod.shape
    tn = 2048
    return pl.pallas_call(
        _mods_kernel,
        out_shape=jax.ShapeDtypeStruct((depth, V7X_SUBLANES, n6), F32),
        grid=(depth, n6 // tn),
        in_specs=[pl.BlockSpec((V7X_SUBLANES, d), lambda l, j: (0, 0)),
                  pl.BlockSpec((1, d, tn), lambda l, j: (l, 0, j)),
                  pl.BlockSpec((1, 1, tn), lambda l, j: (l, 0, j))],
        out_specs=pl.BlockSpec((1, V7X_SUBLANES, tn), lambda l, j: (l, 0, j)),
        compiler_params=_params(("parallel", "parallel")),
        name="mods",
    )(cc, w_mod, b_mod.reshape(depth, 1, n6))


def _in_proj_kernel(x_ref, xp_ref, xn_ref, mod_ref, g_ref, w_ref, cw_ref, cb_ref,
                    q_ref, kv_ref, sgu_ref, z_ref, xbc_ref, dt_ref):
    i = pl.program_id(1)
    shift = mod_ref[0, 0:1, :]
    scale = mod_ref[0, 1:2, :]

    def norm_mod(t):
        return _rms(t, g_ref[...]) * (1.0 + scale) + shift

    a, b = SEG_XBC
    halo = jnp.concatenate([norm_mod(xp_ref[0]), norm_mod(xn_ref[0])], axis=0).astype(BF16)
    pre_halo = _dot(halo, w_ref[:, a:b])
    prev_row = jnp.where(i > 0, pre_halo[V7X_SUBLANES - 1:V7X_SUBLANES, :], 0.0)
    next_row = jnp.where(i < pl.num_programs(1) - 1, pre_halo[V7X_SUBLANES:V7X_SUBLANES + 1, :], 0.0)

    h = norm_mod(x_ref[0]).astype(BF16)
    tm = h.shape[0]
    pre = _dot(h, w_ref[:, a:b])
    row = lax.broadcasted_iota(jnp.int32, pre.shape, 0)
    up = jnp.where(row == 0, prev_row, pltpu.roll(pre, 1, axis=0))
    dn = jnp.where(row == tm - 1, next_row, pltpu.roll(pre, tm - 1, axis=0))
    conv = cb_ref[...] + cw_ref[0:1, :] * up + cw_ref[1:2, :] * pre + cw_ref[2:3, :] * dn
    xbc_ref[0] = _silu(conv).astype(xbc_ref.dtype)

    for ref, (a, b) in ((q_ref, SEG_Q), (kv_ref, SEG_KV), (sgu_ref, SEG_SGU), (z_ref, SEG_Z), (dt_ref, SEG_DT)):
        ref[0] = _dot(h, w_ref[:, a:b]).astype(ref.dtype)


def _in_proj(x, mods, mod_row, gain, w_in_p, conv_w, conv_b, layer, tm):
    b, s, d = x.shape
    segs = (SEG_Q, SEG_KV, SEG_SGU, SEG_Z, SEG_XBC, SEG_DT)
    dts = (BF16, BF16, BF16, BF16, BF16, F32)
    per = tm // V7X_SUBLANES
    last_blk = s // V7X_SUBLANES - 1
    return pl.pallas_call(
        _in_proj_kernel,
        out_shape=[jax.ShapeDtypeStruct((b, s, hi - lo), dt) for (lo, hi), dt in zip(segs, dts)],
        grid=(b, s // tm),
        in_specs=[pl.BlockSpec((1, tm, d), lambda bi, i: (bi, i, 0)),
                  pl.BlockSpec((1, V7X_SUBLANES, d), lambda bi, i: (bi, jnp.maximum(i * per - 1, 0), 0)),
                  pl.BlockSpec((1, V7X_SUBLANES, d), lambda bi, i: (bi, jnp.minimum((i + 1) * per, last_blk), 0)),
                  _mod_block(layer, d, mod_row),
                  _layer_block(layer, 1, d),
                  pl.BlockSpec((None, d, P_PAD), lambda bi, i: (layer, 0, 0), pipeline_mode=pl.Buffered(1)),
                  _layer_block(layer, 3, XBC_W),
                  _layer_block(layer, 1, XBC_W)],
        out_specs=[pl.BlockSpec((1, tm, hi - lo), lambda bi, i: (bi, i, 0)) for lo, hi in segs],
        compiler_params=_params(("parallel", "parallel")),
        name="in_proj",
    )(x, x, x, mods, gain, w_in_p, conv_w, conv_b)


def _q_proj_kernel(cq_ref, an_ref, w_ref, g_ref, g2_ref, cs_ref, q_ref):
    cq = cq_ref[0].astype(F32)
    cqn = _rms(cq, an_ref[...]).astype(BF16)
    gcs = g2_ref[...] * cs_ref[...]
    gc, gs = gcs[:, :V7X_LANES], gcs[:, V7X_LANES:]
    lanes = V7X_LANES
    for pair in range(N_HEADS // 2):
        y = _dot(cqn, w_ref[:, pair * 2 * Q_HEAD_COLS:(pair + 1) * 2 * Q_HEAD_COLS])
        for e in range(2):
            h = 2 * pair + e
            a = y[:, e * lanes:(e + 1) * lanes]
            r = y[:, (2 + e) * lanes:(3 + e) * lanes]
            rp = y[:, (4 + e) * lanes:(5 + e) * lanes]
            ssq = jnp.sum(a * a + r * r, axis=-1, keepdims=True)
            rs = lax.rsqrt(ssq * (1.0 / QK) + NORM_EPS) * (SM_SCALE * LOG2_E)
            q_ref[0, :, h * HEAD_PAD:h * HEAD_PAD + NOPE] = (a * g_ref[...] * rs).astype(BF16)
            q_ref[0, :, h * HEAD_PAD + NOPE:(h + 1) * HEAD_PAD] = ((r * gc + rp * gs) * rs).astype(BF16)


def _q_proj(cq, an, w_uq_p, layer, g, g2, cs, tm):
    b, s, _ = cq.shape
    width = N_HEADS * HEAD_PAD
    return pl.pallas_call(
        _q_proj_kernel,
        out_shape=jax.ShapeDtypeStruct((b, s, width), BF16),
        grid=(b, s // tm),
        in_specs=[pl.BlockSpec((1, tm, Q_LORA), lambda bi, i: (bi, i, 0)),
                  _layer_block(layer, 1, Q_LORA),
                  _layer_block(layer, Q_LORA, N_HEADS * Q_HEAD_COLS),
                  _layer_block(layer, 1, NOPE),
                  _layer_block(layer, 1, 2 * V7X_LANES),
                  pl.BlockSpec((tm, 2 * V7X_LANES), lambda bi, i: (i, 0))],
        out_specs=pl.BlockSpec((1, tm, width), lambda bi, i: (bi, i, 0)),
        compiler_params=_params(("parallel", "parallel")),
        name="q_proj",
    )(cq, an, w_uq_p, g, g2, cs)


def _kv_proj_kernel(t_ref, an_ref, w_ref, g_ref, g2_ref, cs_ref, k_ref, v_ref):
    ckv = t_ref[0, :, :KV_LORA].astype(F32)
    kr = t_ref[0, :, KV_LORA:KV_LORA + V7X_LANES].astype(F32)
    krp = t_ref[0, :, KV_LORA + V7X_LANES:].astype(F32)
    ckvn = _rms(ckv, an_ref[...]).astype(BF16)
    gcs = g2_ref[...] * cs_ref[...]
    rot = kr * gcs[:, :V7X_LANES] + krp * gcs[:, V7X_LANES:]
    kr_sq = kr * kr
    lane = lax.broadcasted_iota(jnp.int32, kr.shape, 1)
    ones_col = jnp.where(lane == 0, 1.0, 0.0).astype(BF16)
    for h in range(N_HEADS):
        y = _dot(ckvn, w_ref[:, h * 2 * NOPE:(h + 1) * 2 * NOPE])
        kn = y[:, :NOPE]
        rs = lax.rsqrt(jnp.sum(kn * kn + kr_sq, axis=-1, keepdims=True) * (1.0 / QK) + NORM_EPS)
        k_ref[0, :, h * HEAD_PAD:h * HEAD_PAD + NOPE] = (kn * g_ref[...] * rs).astype(BF16)
        k_ref[0, :, h * HEAD_PAD + NOPE:(h + 1) * HEAD_PAD] = (rot * rs).astype(BF16)
        v_ref[0, :, h * HEAD_PAD:h * HEAD_PAD + V_DIM] = y[:, NOPE:].astype(BF16)
        v_ref[0, :, h * HEAD_PAD + V_DIM:(h + 1) * HEAD_PAD] = ones_col


def _kv_proj(t, an, w_ukv, layer, g, g2, cs, tm):
    b, s, wt = t.shape
    return pl.pallas_call(
        _kv_proj_kernel,
        out_shape=[jax.ShapeDtypeStruct((b, s, N_HEADS * HEAD_PAD), BF16),
                   jax.ShapeDtypeStruct((b, s, N_HEADS * HEAD_PAD), BF16)],
        grid=(b, s // tm),
        in_specs=[pl.BlockSpec((1, tm, wt), lambda bi, i: (bi, i, 0)),
                  _layer_block(layer, 1, KV_LORA),
                  _layer_block(layer, KV_LORA, N_HEADS * 2 * NOPE),
                  _layer_block(layer, 1, NOPE),
                  _layer_block(layer, 1, 2 * V7X_LANES),
                  pl.BlockSpec((tm, 2 * V7X_LANES), lambda bi, i: (i, 0))],
        out_specs=[pl.BlockSpec((1, tm, N_HEADS * HEAD_PAD), lambda bi, i: (bi, i, 0)),
                   pl.BlockSpec((1, tm, N_HEADS * HEAD_PAD), lambda bi, i: (bi, i, 0))],
        compiler_params=_params(("parallel", "parallel")),
        name="kv_proj",
    )(t, an, w_ukv, g, g2, cs)


def _attn_kernel(*refs, layer, n_lat, tk, sub):
    if n_lat:
        bounded_ref, q_ref, kc_ref, vc_ref, kl_ref, vl_ref, o_ref, s0_ref, s1_ref = refs
    else:
        bounded_ref, q_ref, kc_ref, vc_ref, o_ref = refs
    n_chunks = n_lat // tk if n_lat else 0
    sub_tiles = [slice(r, r + sub) for r in range(0, q_ref.shape[1], sub)]

    def finish(acc, rows):
        o_ref[0, rows, :] = (acc[:, :V_DIM] * (1.0 / acc[:, V_DIM:V_DIM + 1])).astype(o_ref.dtype)

    @pl.when(bounded_ref[layer] != 0)
    def _():
        for rows in sub_tiles:
            q = q_ref[0, rows, :]
            acc = _dot(jnp.exp2(_dot_nt(q, kc_ref[0])).astype(BF16), vc_ref[0])
            for j in range(n_chunks):
                p = jnp.exp2(_dot_nt(q, kl_ref[0, j * tk:(j + 1) * tk, :])).astype(BF16)
                acc = acc + _dot(p, vl_ref[0, j * tk:(j + 1) * tk, :])
            finish(acc, rows)

    @pl.when(bounded_ref[layer] == 0)
    def _():
        def update(carry, s, v):
            m, acc = carry
            m_new = jnp.maximum(m, jnp.max(s, axis=-1, keepdims=True))
            p = jnp.exp2(s - m_new).astype(BF16)
            return m_new, jnp.exp2(m - m_new) * acc + _dot(p, v)

        for rows in sub_tiles:
            q = q_ref[0, rows, :]
            carry = (jnp.full((sub, 1), -jnp.inf, F32), jnp.zeros((sub, HEAD_PAD), F32))
            if n_lat:
                slots = (s0_ref, s1_ref)

                def scores(j):
                    slots[j % 2][...] = _dot_nt(q, kl_ref[0, j * tk:(j + 1) * tk, :])

                scores(0)
                carry = update(carry, _dot_nt(q, kc_ref[0]), vc_ref[0])
                for j in range(n_chunks):
                    if j + 1 < n_chunks:
                        scores(j + 1)
                    carry = update(carry, slots[j % 2][...], vl_ref[0, j * tk:(j + 1) * tk, :])
            else:
                carry = update(carry, _dot_nt(q, kc_ref[0]), vc_ref[0])
            finish(carry[1], rows)


def _attention(bounded, layer, q, k_c, v_c, k_l=None, v_l=None, *, tq, tk, sub):
    b, s, _ = q.shape
    sub = min(sub, tq)
    assert tq % sub == 0
    n_ctx = k_c.shape[1]
    n_lat = 0 if k_l is None else k_l.shape[1]
    in_specs = [pl.BlockSpec(memory_space=pltpu.SMEM),
                pl.BlockSpec((1, tq, HEAD_PAD), lambda bi, h, i: (bi, i, h)),
                pl.BlockSpec((1, n_ctx, HEAD_PAD), lambda bi, h, i: (bi, 0, h)),
                pl.BlockSpec((1, n_ctx, HEAD_PAD), lambda bi, h, i: (bi, 0, h))]
    args = [bounded, q, k_c, v_c]
    scratch = []
    if n_lat:
        tk = min(tk, n_lat)
        assert n_lat % tk == 0
        in_specs += [pl.BlockSpec((1, n_lat, HEAD_PAD), lambda bi, h, i: (bi, 0, h)),
                     pl.BlockSpec((1, n_lat, HEAD_PAD), lambda bi, h, i: (bi, 0, h))]
        args += [k_l, v_l]
        scratch = [pltpu.VMEM((sub, tk), F32), pltpu.VMEM((sub, tk), F32)]
    return pl.pallas_call(
        functools.partial(_attn_kernel, layer=layer, n_lat=n_lat, tk=tk, sub=sub),
        out_shape=jax.ShapeDtypeStruct((b, s, ATT_W), BF16),
        grid=(b, N_HEADS, s // tq),
        in_specs=in_specs,
        out_specs=pl.BlockSpec((1, tq, V_DIM), lambda bi, h, i: (bi, i, h)),
        scratch_shapes=scratch,
        compiler_params=_params(("parallel", "parallel", "arbitrary")),
        name="attention",
    )(*args)


def _sgu_kernel(p_ref, gn_ref, w_ref, b_ref, o_ref, *, n_chunks):
    for c in range(n_chunks):
        rows = slice(c * SGU_CH, (c + 1) * SGU_CH)
        z = _gelu_tanh(p_ref[0, rows, :].astype(F32))
        u = z[:, :SGU_W]
        vn = _rms(z[:, SGU_W:], gn_ref[...]).astype(BF16)
        for g in range(SGU_G):
            cols = slice(g * SGU_CH, (g + 1) * SGU_CH)
            mixed = _dot(w_ref[g], vn[:, cols]) + b_ref[g]
            o_ref[0, rows, cols] = (u[:, cols] * mixed).astype(o_ref.dtype)


def _sgu(p, gn, w_s, b_b, layer, rows):
    b, s, _ = p.shape
    return pl.pallas_call(
        functools.partial(_sgu_kernel, n_chunks=rows // SGU_CH),
        out_shape=jax.ShapeDtypeStruct((b, s, SGU_W), BF16),
        grid=(b, s // rows),
        in_specs=[pl.BlockSpec((1, rows, 2 * SGU_W), lambda bi, i: (bi, i, 0)),
                  _layer_block(layer, 1, SGU_W),
                  _layer_block(layer, SGU_G, SGU_CH, SGU_CH),
                  _layer_block(layer, SGU_G, SGU_CH, SGU_CH)],
        out_specs=pl.BlockSpec((1, rows, SGU_W), lambda bi, i: (bi, i, 0)),
        compiler_params=_params(("parallel", "parallel")),
        name="sgu",
    )(p, gn, w_s, b_b)


def _ssd_direction(x_ref, dt_ref, bias_ref, alog_ref, tri_ref, exp_ref, state_ref, y_ref, *, reverse):
    q = SSD_Q
    xs = x_ref[0, :, :SSD_IN].astype(F32)
    dt = _softplus(dt_ref[0] + bias_ref[...])
    a_dt = dt * (-jnp.exp(alog_ref[...]))
    tri = tri_ref[...]
    acs = sum(_dot(tri, part) for part in _split3(a_dt))
    acs_t = acs.T
    expand = exp_ref[...]
    acs_x = sum(_dot(part, expand) for part in _split3(acs))
    dt_hi = dt.astype(BF16)
    dt_x = _dot(dt_hi, expand) + _dot((dt - dt_hi.astype(F32)).astype(BF16), expand)
    last = acs_x[0:1, :] if reverse else acs_x[q - 1:q, :]
    xdt = xs * dt_x
    x_end = (xdt * jnp.exp(last - acs_x)).astype(BF16)
    decay_out = jnp.exp(acs_x)
    chunk_decay = jnp.exp(last)

    ti = lax.broadcasted_iota(jnp.int32, (q, q), 0)
    tj = lax.broadcasted_iota(jnp.int32, (q, q), 1)
    keep = (ti <= tj) if reverse else (ti >= tj)
    lane_blk = lax.shift_right_logical(lax.broadcasted_iota(jnp.int32, (q, GRP_W), 1), int(math.log2(SSD_P)))
    head0 = SSD_H if reverse else 0
    heads_per_group = SSD_H // SSD_G
    for g in range(SSD_G):
        bm = x_ref[0, :, SSD_IN + g * SSD_N:SSD_IN + (g + 1) * SSD_N]
        cm = x_ref[0, :, SSD_IN + (SSD_G + g) * SSD_N:SSD_IN + (SSD_G + g + 1) * SSD_N]
        cb = _dot_nt(cm, bm)
        cols = slice(g * GRP_W, (g + 1) * GRP_W)
        state = state_ref[g]
        y_off = _dot(cm, state.astype(BF16)) * decay_out[:, cols]
        state_ref[g] = state * chunk_decay[:, cols] + _dot(bm.astype(F32).T.astype(BF16), x_end[:, cols])
        blocks, stacked = [], []
        for hh in range(heads_per_group):
            hcol = head0 + g * heads_per_group + hh
            seg = acs[:, hcol:hcol + 1] - acs_t[hcol:hcol + 1, :]
            blocks.append((cb * jnp.exp(jnp.where(keep, seg, -jnp.inf))).astype(BF16))
            stacked.append(jnp.where(lane_blk == hh, xdt[:, cols], 0.0).astype(BF16))
        y = y_off + _dot(jnp.concatenate(blocks, axis=1), jnp.concatenate(stacked, axis=0))
        y_ref[0, :, cols] = y.astype(y_ref.dtype)


def _ssd_kernel(xf_ref, dtf_ref, xb_ref, dtb_ref, bias_ref, alog_ref, tri_ref, exp_ref, init_ref,
                yf_ref, yb_ref, fin_ref, state_ref):
    s = pl.program_id(1)

    @pl.when(s == 0)
    def _():
        state_ref[...] = init_ref[0]

    _ssd_direction(xf_ref, dtf_ref, bias_ref, alog_ref, tri_ref.at[0], exp_ref.at[0], state_ref.at[0], yf_ref,
                   reverse=False)
    _ssd_direction(xb_ref, dtb_ref, bias_ref, alog_ref, tri_ref.at[1], exp_ref.at[1], state_ref.at[1], yb_ref,
                   reverse=True)

    @pl.when(s == pl.num_programs(1) - 1)
    def _():
        fin_ref[0] = state_ref[...]


def _ssd(xbc, dt, bias, alog, tri, expand, init, layer):
    b, s, _ = xbc.shape
    nc = s // SSD_Q

    def main_f(bi, i): return (bi, i, 0)
    def main_b(bi, i): return (bi, nc - 1 - i, 0)
    const2 = lambda bi, i: (0, 0)
    const3 = lambda bi, i: (0, 0, 0)
    state_shape = (2, SSD_G, SSD_N, GRP_W)
    return pl.pallas_call(
        _ssd_kernel,
        out_shape=[jax.ShapeDtypeStruct((b, s, SSD_IN), BF16), jax.ShapeDtypeStruct((b, s, SSD_IN), BF16),
                   jax.ShapeDtypeStruct((b,) + state_shape, F32)],
        grid=(b, nc),
        in_specs=[pl.BlockSpec((1, SSD_Q, XBC_W), main_f), pl.BlockSpec((1, SSD_Q, V7X_LANES), main_f),
                  pl.BlockSpec((1, SSD_Q, XBC_W), main_b), pl.BlockSpec((1, SSD_Q, V7X_LANES), main_b),
                  _layer_block(layer, 1, V7X_LANES), _layer_block(layer, 1, V7X_LANES),
                  pl.BlockSpec((2, SSD_Q, SSD_Q), const3), pl.BlockSpec((2, V7X_LANES, SSD_IN), const3),
                  pl.BlockSpec((1,) + state_shape, lambda bi, i: (bi, 0, 0, 0, 0))],
        out_specs=[pl.BlockSpec((1, SSD_Q, SSD_IN), main_f), pl.BlockSpec((1, SSD_Q, SSD_IN), main_b),
                   pl.BlockSpec((1,) + state_shape, lambda bi, i: (bi, 0, 0, 0, 0))],
        scratch_shapes=[pltpu.VMEM(state_shape, F32)],
        compiler_params=_params(("parallel", "arbitrary")),
        name="ssd",
    )(xbc, dt, xbc, dt, bias, alog, tri, expand, init)


def _out_proj_kernel(x_ref, mod_ref, att_ref, sgu_ref, yf_ref, yb_ref, xs_ref, z_ref,
                     ga_ref, gs_ref, d_ref, gy_ref, gf_ref, w_ref, o_ref, h_ref):
    att = _rms(att_ref[0].astype(F32), ga_ref[...]).astype(BF16)
    sgu = _rms(sgu_ref[0].astype(F32), gs_ref[...]).astype(BF16)
    y = yf_ref[0].astype(F32) + yb_ref[0].astype(F32) + d_ref[...] * xs_ref[0].astype(F32)
    ssd = _rms(y * _silu(z_ref[0].astype(F32)), gy_ref[...]).astype(BF16)
    mix = (_dot(att, w_ref[:ATT_W, :]) + _dot(sgu, w_ref[ATT_W:ATT_W + SGU_W, :])
           + _dot(ssd, w_ref[ATT_W + SGU_W:, :]))
    x_new = x_ref[0] + mod_ref[0, 2:3, :] * mix
    o_ref[0] = x_new
    h_ref[0] = (_rms(x_new, gf_ref[...]) * (1.0 + mod_ref[0, 4:5, :]) + mod_ref[0, 3:4, :]).astype(BF16)


def _out_proj(x, mods, mod_row, att, sgu, yf, yb, xs, z, ga, gs, dvec, gy, gf, w_out, layer, tm):
    b, s, d = x.shape
    row = lambda w: pl.BlockSpec((1, tm, w), lambda bi, i: (bi, i, 0))
    vec = lambda w: _layer_block(layer, 1, w)
    return pl.pallas_call(
        _out_proj_kernel,
        out_shape=[jax.ShapeDtypeStruct((b, s, d), F32), jax.ShapeDtypeStruct((b, s, d), BF16)],
        grid=(b, s // tm),
        in_specs=[row(d), _mod_block(layer, d, mod_row),
                  row(ATT_W), row(SGU_W), row(SSD_IN), row(SSD_IN), row(SSD_IN), row(SSD_IN),
                  vec(ATT_W), vec(SGU_W), vec(SSD_IN), vec(SSD_IN), vec(d),
                  pl.BlockSpec((None,) + w_out.shape[1:], lambda bi, i: (layer, 0, 0),
                               pipeline_mode=pl.Buffered(1))],
        out_specs=[row(d), row(d)],
        compiler_params=_params(("parallel", "parallel")),
        name="out_proj",
    )(x, mods, att, sgu, yf, yb, xs, z, ga, gs, dvec, gy, gf, w_out)


def _ffn_kernel(x_ref, h_ref, hp_ref, hn_ref, mod_ref, wg_ref, wu_ref, cw_ref, cb_ref, wd_ref, o_ref,
                halo_sc, acc_sc, *, seq_rows):
    i = pl.program_id(1)
    j = pl.program_id(2)
    hb = BF16_ROWS

    @pl.when(j == 0)
    def _():
        halo_sc[0:hb, :] = jnp.where(i > 0, hp_ref[0], jnp.zeros_like(hp_ref[0]))
        halo_sc[hb:, :] = jnp.where(i < pl.num_programs(1) - 1, hn_ref[0], jnp.zeros_like(hn_ref[0]))
        acc_sc[...] = jnp.zeros_like(acc_sc)

    h = h_ref[0]
    tm = h.shape[0]
    gate = _dot(h, wg_ref[...])
    gate_halo = _dot(halo_sc[...], wg_ref[...])
    row = lax.broadcasted_iota(jnp.int32, gate.shape, 0)
    up = jnp.where(row == 0, gate_halo[hb - 1:hb, :], pltpu.roll(gate, 1, axis=0))
    dn = jnp.where(row == tm - 1, gate_halo[hb:hb + 1, :], pltpu.roll(gate, tm - 1, axis=0))
    if seq_rows is not None:
        pos = lax.rem(row, seq_rows)
        up = jnp.where(pos == 0, 0.0, up)
        dn = jnp.where(pos == seq_rows - 1, 0.0, dn)
    conv = cb_ref[...] + cw_ref[0:1, :] * up + cw_ref[1:2, :] * gate + cw_ref[2:3, :] * dn
    act = (_silu(conv) * _dot(h, wu_ref[...])).astype(BF16)
    acc_sc[...] += _dot(act, wd_ref[...])

    @pl.when(j == pl.num_programs(2) - 1)
    def _():
        o_ref[0] = x_ref[0] + mod_ref[0, 5:6, :] * acc_sc[...]


def _ffn(x, h, mods, mod_row, w_gate, w_up, conv_w, conv_b, w_down, layer, tm, tf, pack_sequences=False):
    out_shape = x.shape
    seq_rows = None
    if pack_sequences:
        assert mod_row is not None
        seq_rows, d = x.shape[1], x.shape[2]
        x, h = x.reshape(1, -1, d), h.reshape(1, -1, d)
        tm = x.shape[1]
    b, s, d = x.shape
    f = w_down.shape[1]
    per = tm // BF16_ROWS
    last_blk = s // BF16_ROWS - 1
    assert f % tf == 0
    return pl.pallas_call(
        functools.partial(_ffn_kernel, seq_rows=seq_rows),
        out_shape=jax.ShapeDtypeStruct((b, s, d), F32),
        grid=(b, s // tm, f // tf),
        in_specs=[pl.BlockSpec((1, tm, d), lambda bi, i, j: (bi, i, 0)),
                  pl.BlockSpec((1, tm, d), lambda bi, i, j: (bi, i, 0)),
                  pl.BlockSpec((1, BF16_ROWS, d), lambda bi, i, j: (bi, jnp.maximum(i * per - 1, 0), 0)),
                  pl.BlockSpec((1, BF16_ROWS, d), lambda bi, i, j: (bi, jnp.minimum((i + 1) * per, last_blk), 0)),
                  _mod_block(layer, d, mod_row),
                  pl.BlockSpec((None, d, tf), lambda bi, i, j: (layer, 0, j)),
                  pl.BlockSpec((None, d, tf), lambda bi, i, j: (layer, 0, j)),
                  pl.BlockSpec((None, 3, tf), lambda bi, i, j: (layer, 0, j)),
                  pl.BlockSpec((None, 1, tf), lambda bi, i, j: (layer, 0, j)),
                  pl.BlockSpec((None, tf, d), lambda bi, i, j: (layer, j, 0))],
        out_specs=pl.BlockSpec((1, tm, d), lambda bi, i, j: (bi, i, 0)),
        scratch_shapes=[pltpu.VMEM((2 * BF16_ROWS, d), BF16), pltpu.VMEM((tm, d), F32)],
        compiler_params=_params(("parallel", "parallel", "arbitrary")),
        name="ffn",
    )(x, h, h, h, mods, w_gate, w_up, conv_w, conv_b, w_down).reshape(out_shape)


def _rope_tables(seq):
    rows = seq // GRID_COLS
    pairs = ROPE // 4
    freqs = ROPE_THETA ** (-jnp.arange(pairs, dtype=F32) / pairs)
    ar = jnp.arange(rows, dtype=F32)[:, None] * freqs
    ac = jnp.arange(GRID_COLS, dtype=F32)[:, None] * freqs
    by_row = lambda t: jnp.repeat(t, GRID_COLS, axis=0)
    by_col = lambda t: jnp.tile(t, (rows, 1))
    cr, sr, cc, sc = by_row(jnp.cos(ar)), by_row(jnp.sin(ar)), by_col(jnp.cos(ac)), by_col(jnp.sin(ac))
    cos = jnp.concatenate([cr, cr, cc, cc], axis=-1)
    sin = jnp.concatenate([-sr, sr, -sc, sc], axis=-1)
    zeros = jnp.zeros_like(cos)
    return jnp.concatenate([cos, zeros, sin, zeros], axis=-1)


def _pack_w_in(w):
    w = w.astype(BF16)
    off_kr, off_sgu = Q_LORA + KV_LORA, Q_LORA + KV_LORA + ROPE
    off_dt = off_sgu + 2 * SGU_W + SSD_IN + XBC_W
    kr = w[..., off_kr:off_sgu]
    dt = w[..., off_dt:]
    z64 = jnp.zeros(w.shape[:2] + (V7X_LANES - ROPE,), BF16)
    return jnp.concatenate([w[..., :off_kr], kr, z64, kr[..., _ROT_PERM], z64, w[..., off_sgu:off_dt], dt,
                            jnp.zeros(w.shape[:2] + (V7X_LANES - dt.shape[-1],), BF16)], axis=-1)


def _pack_w_uq(w):
    depth = w.shape[0]
    pairs = N_HEADS // 2
    w = w.astype(BF16).reshape(depth, Q_LORA, pairs, 2, QK)
    rot = w[..., NOPE:]
    z64 = jnp.zeros((depth, Q_LORA, pairs, 2, V7X_LANES - ROPE), BF16)
    both = lambda t: t.reshape(depth, Q_LORA, pairs, 2 * V7X_LANES)
    return jnp.concatenate([both(w[..., :NOPE]), both(jnp.concatenate([rot, z64], axis=-1)),
                            both(jnp.concatenate([rot[..., _ROT_PERM], z64], axis=-1))],
                           axis=-1).reshape(depth, Q_LORA, N_HEADS * Q_HEAD_COLS)


def _rot_gains(g):
    rot = g[:, NOPE:]
    z64 = jnp.zeros((g.shape[0], V7X_LANES - ROPE), g.dtype)
    return jnp.concatenate([rot, z64, rot[:, _ROT_PERM], z64], axis=-1)[:, None, :]


def _row_vectors(v, width=None):
    v = v.reshape(v.shape[0], 1, -1)
    return v if width is None else jnp.pad(v, ((0, 0), (0, 0), (0, width - v.shape[-1])))


def kernel(x, c, ctx, c_ctx, w_mod, b_mod, norm_mix, norm_ffn, w_in, q_a_norm, w_uq, kv_a_norm, w_ukv, q_norm, k_norm, attn_out_norm, sgu_norm, sgu_w, sgu_b, gmlp_out_norm, ssd_conv_w, ssd_conv_b, ssd_dt_bias, ssd_a_log, ssd_d, ssd_norm, w_out, ffn_w_gate, ffn_w_up, ffn_conv_w, ffn_conv_b, ffn_w_down):
    batch, seq, d = x.shape
    n_ctx = ctx.shape[1]
    depth = w_mod.shape[0]
    tm = min(512, seq)
    tm_ctx = min(256, n_ctx)
    tf = 512
    tk_att = 1024
    tq_att = 2048
    sub_att = 1024

    cc = jnp.concatenate([c, c_ctx[None], jnp.zeros((V7X_SUBLANES - batch - 1, d), F32)], axis=0)
    mods = jnp.pad(_mods(cc, w_mod, b_mod).reshape(depth, V7X_SUBLANES, 6, d),
                   ((0, 0), (0, 0), (0, V7X_SUBLANES - 6), (0, 0)))
    ctx_row = batch

    cs_lat = _rope_tables(seq)
    cs_ctx = jnp.concatenate([jnp.ones((n_ctx, ROPE), F32), jnp.zeros((n_ctx, 2 * V7X_LANES - ROPE), F32)], axis=-1)

    ti = np.arange(SSD_Q)
    tri = jnp.asarray(np.stack([ti[:, None] >= ti[None, :], ti[:, None] <= ti[None, :]]), BF16)
    lane_head = np.arange(SSD_IN) // SSD_P
    col = np.arange(V7X_LANES)
    expand = jnp.asarray(np.stack([col[:, None] == lane_head[None, :],
                                   col[:, None] == lane_head[None, :] + SSD_H]), BF16)

    w_in_p, w_uq_p = _pack_w_in(w_in), _pack_w_uq(w_uq)
    w_ukv_b, w_out_b = w_ukv.astype(BF16), w_out.astype(BF16)
    wg, wu, wd = ffn_w_gate.astype(BF16), ffn_w_up.astype(BF16), ffn_w_down.astype(BF16)

    g_mix, g_ffn = _row_vectors(norm_mix), _row_vectors(norm_ffn)
    an_q, an_kv = _row_vectors(q_a_norm), _row_vectors(kv_a_norm)
    gq1, gk1 = _row_vectors(q_norm[:, :NOPE]), _row_vectors(k_norm[:, :NOPE])
    gq2, gk2 = _rot_gains(q_norm), _rot_gains(k_norm)
    g_att, g_sgu_in, g_sgu_out = _row_vectors(attn_out_norm), _row_vectors(sgu_norm), _row_vectors(gmlp_out_norm)
    g_ssd = _row_vectors(ssd_norm)
    sgu_w_b = sgu_w.astype(BF16)
    sgu_b_b = jnp.broadcast_to(sgu_b[..., None], sgu_b.shape + (SGU_CH,))
    conv_b = _row_vectors(ssd_conv_b)
    bias, alog = _row_vectors(ssd_dt_bias, V7X_LANES), _row_vectors(ssd_a_log, V7X_LANES)
    dvec = _row_vectors(jnp.repeat(ssd_d, SSD_P, axis=1))
    fcb = _row_vectors(ffn_conv_b)
    score_bound = (QK * SM_SCALE * LOG2_E) * jnp.max(jnp.abs(q_norm), axis=1) * jnp.max(jnp.abs(k_norm), axis=1)
    bounded = (score_bound <= SCORE_BOUND_LIMIT).astype(jnp.int32)

    x_lat, x_ctx = x, ctx
    for l in range(depth):
        need_ctx = l < depth - 1

        def mixers(xs_in, mod_row, cs, t_rows, init_state, k_c=None, v_c=None, outputs=True):
            pq, pkv, psgu, pz, xbc, pdt = _in_proj(xs_in, mods, mod_row, g_mix, w_in_p, ssd_conv_w, conv_b, l, t_rows)
            k, v = _kv_proj(pkv, an_kv, w_ukv_b, l, gk1, gk2, cs, t_rows)
            yf, yb, fin = _ssd(xbc, pdt, bias, alog, tri, expand, init_state, l)
            if not outputs:
                return None, k, v, fin
            q = _q_proj(pq, an_q, w_uq_p, l, gq1, gq2, cs, t_rows)
            if k_c is None:
                att = _attention(bounded, l, q, k, v, tq=t_rows, tk=tk_att, sub=sub_att)
            else:
                att = _attention(bounded, l, q, k_c, v_c, k, v, tq=min(tq_att, q.shape[1]), tk=tk_att, sub=sub_att)
            sgu = _sgu(psgu, g_sgu_in, sgu_w_b, sgu_b_b, l, t_rows)
            out = _out_proj(xs_in, mods, mod_row, att, sgu, yf, yb, xbc, pz, g_att, g_sgu_out, dvec, g_ssd, g_ffn,
                            w_out_b, l, t_rows)
            return out, k, v, fin

        zero_state = jnp.zeros((batch, 2, SSD_G, SSD_N, GRP_W), F32)
        ctx_mid, k_c, v_c, ctx_state = mixers(x_ctx, ctx_row, cs_ctx, tm_ctx, zero_state, outputs=need_ctx)
        (x_lat, h_lat), _, _, _ = mixers(x_lat, None, cs_lat, tm, ctx_state, k_c, v_c)
        x_lat = _ffn(x_lat, h_lat, mods, None, wg, wu, ffn_conv_w, fcb, wd, l, tm, tf)
        if need_ctx:
            x_ctx = _ffn(ctx_mid[0], ctx_mid[1], mods, ctx_row, wg, wu, ffn_conv_w, fcb, wd, l, tm_ctx, tf,
                         pack_sequences=True)
    return x_lat
```

```python
import functools
import math

import jax
import jax.numpy as jnp
import numpy as np
from jax import lax
from jax.experimental import pallas as pl
from jax.experimental.pallas import tpu as pltpu

F32 = jnp.float32
BF16 = jnp.bfloat16
NORM_EPS = 1e-6

GRID_COLS = 64
N_HEADS = 8
NOPE = 128
ROPE = 64
QK = NOPE + ROPE
V_DIM = 128
Q_LORA = 768
KV_LORA = 512
ROPE_THETA = 10000.0
SM_SCALE = 1.0 / math.sqrt(QK)
LOG2_E = math.log2(math.e)
SCORE_BOUND_LIMIT = 60.0
HEAD_PAD = 256
ATT_W = N_HEADS * V_DIM
SGU_G = 4
SGU_CH = 128
SGU_W = 512
SSD_H = 8
SSD_P = 64
SSD_IN = SSD_H * SSD_P
SSD_N = 128
SSD_G = 2
SSD_Q = 128
XBC_W = SSD_IN + 2 * SSD_G * SSD_N
GRP_W = (SSD_H // SSD_G) * SSD_P

V7X_LANES = 128
V7X_SUBLANES = 8
BF16_ROWS = 2 * V7X_SUBLANES
V7X_VMEM_BYTES = 64 * 1024 * 1024
VMEM_LIMIT = V7X_VMEM_BYTES - 8 * 1024 * 1024

SEG_Q = (0, Q_LORA)
SEG_KV = (SEG_Q[1], SEG_Q[1] + KV_LORA + 2 * V7X_LANES)
SEG_SGU = (SEG_KV[1], SEG_KV[1] + 2 * SGU_W)
SEG_Z = (SEG_SGU[1], SEG_SGU[1] + SSD_IN)
SEG_XBC = (SEG_Z[1], SEG_Z[1] + XBC_W)
SEG_DT = (SEG_XBC[1], SEG_XBC[1] + V7X_LANES)
P_PAD = SEG_DT[1]
Q_HEAD_COLS = NOPE + 2 * V7X_LANES

_ROT_PERM = np.concatenate([np.arange(16, 32), np.arange(0, 16), np.arange(48, 64), np.arange(32, 48)])


def _params(sem, vmem=VMEM_LIMIT):
    return pltpu.CompilerParams(dimension_semantics=sem, vmem_limit_bytes=vmem)


def _layer_block(layer, *shape):
    zeros = (0,) * len(shape)
    return pl.BlockSpec((None,) + shape, lambda *_: (layer,) + zeros)


def _mod_block(layer, d, fixed_row):
    if fixed_row is None:
        return pl.BlockSpec((None, 1, V7X_SUBLANES, d), lambda bi, *_: (layer, bi, 0, 0))
    return pl.BlockSpec((None, 1, V7X_SUBLANES, d), lambda *_: (layer, fixed_row, 0, 0))


def _dot(a, b):
    return jnp.dot(a, b, preferred_element_type=F32)


def _dot_nt(a, b):
    return lax.dot_general(a, b, (((1,), (1,)), ((), ())), preferred_element_type=F32)


def _rms(t, gain):
    return t * lax.rsqrt(jnp.mean(t * t, axis=-1, keepdims=True) + NORM_EPS) * gain


def _silu(t):
    return t * (1.0 / (1.0 + jnp.exp(-t)))


def _gelu_tanh(t):
    return 0.5 * t * (1.0 + jnp.tanh(math.sqrt(2.0 / math.pi) * (t + 0.044715 * (t * t * t))))


def _softplus(t):
    return jnp.maximum(t, 0.0) + jnp.log1p(jnp.exp(-jnp.abs(t)))


def _split3(t):
    hi = t.astype(BF16)
    r1 = t - hi.astype(F32)
    mid = r1.astype(BF16)
    lo = (r1 - mid.astype(F32)).astype(BF16)
    return hi, mid, lo


def _mods_kernel(c_ref, w_ref, b_ref, o_ref):
    s = _silu(c_ref[...])
    s_hi = s.astype(BF16)
    s_lo = (s - s_hi.astype(F32)).astype(BF16)
    w = w_ref[0]
    w_hi = w.astype(BF16)
    w_lo = (w - w_hi.astype(F32)).astype(BF16)
    o_ref[0] = _dot(s_hi, w_hi) + _dot(s_lo, w_hi) + _dot(s_hi, w_lo) + b_ref[0]


def _mods(cc, w_mod, b_mod):
    depth, d, n6 = w_mod.shape
    tn = 1024
    return pl.pallas_call(
        _mods_kernel,
        out_shape=jax.ShapeDtypeStruct((depth, V7X_SUBLANES, n6), F32),
        grid=(depth, n6 // tn),
        in_specs=[pl.BlockSpec((V7X_SUBLANES, d), lambda l, j: (0, 0)),
                  pl.BlockSpec((1, d, tn), lambda l, j: (l, 0, j)),
                  pl.BlockSpec((1, 1, tn), lambda l, j: (l, 0, j))],
        out_specs=pl.BlockSpec((1, V7X_SUBLANES, tn), lambda l, j: (l, 0, j)),
        compiler_params=_params(("parallel", "parallel")),
        name="mods",
    )(cc, w_mod, b_mod.reshape(depth, 1, n6))


def _in_proj_kernel(x_ref, xp_ref, xn_ref, mod_ref, g_ref, w_ref, cw_ref, cb_ref,
                    q_ref, kv_ref, sgu_ref, z_ref, xbc_ref, dt_ref):
    i = pl.program_id(1)
    shift = mod_ref[0, 0:1, :]
    scale = mod_ref[0, 1:2, :]

    def norm_mod(t):
        return _rms(t, g_ref[...]) * (1.0 + scale) + shift

    a, b = SEG_XBC
    halo = jnp.concatenate([norm_mod(xp_ref[0]), norm_mod(xn_ref[0])], axis=0).astype(BF16)
    pre_halo = _dot(halo, w_ref[:, a:b])
    prev_row = jnp.where(i > 0, pre_halo[V7X_SUBLANES - 1:V7X_SUBLANES, :], 0.0)
    next_row = jnp.where(i < pl.num_programs(1) - 1, pre_halo[V7X_SUBLANES:V7X_SUBLANES + 1, :], 0.0)

    h = norm_mod(x_ref[0]).astype(BF16)
    tm = h.shape[0]
    pre = _dot(h, w_ref[:, a:b])
    row = lax.broadcasted_iota(jnp.int32, pre.shape, 0)
    up = jnp.where(row == 0, prev_row, pltpu.roll(pre, 1, axis=0))
    dn = jnp.where(row == tm - 1, next_row, pltpu.roll(pre, tm - 1, axis=0))
    conv = cb_ref[...] + cw_ref[0:1, :] * up + cw_ref[1:2, :] * pre + cw_ref[2:3, :] * dn
    xbc_ref[0] = _silu(conv).astype(xbc_ref.dtype)

    for ref, (a, b) in ((q_ref, SEG_Q), (kv_ref, SEG_KV), (sgu_ref, SEG_SGU), (z_ref, SEG_Z), (dt_ref, SEG_DT)):
        ref[0] = _dot(h, w_ref[:, a:b]).astype(ref.dtype)


def _in_proj(x, mods, mod_row, gain, w_in_p, conv_w, conv_b, layer, tm):
    b, s, d = x.shape
    segs = (SEG_Q, SEG_KV, SEG_SGU, SEG_Z, SEG_XBC, SEG_DT)
    dts = (BF16, BF16, BF16, BF16, BF16, F32)
    per = tm // V7X_SUBLANES
    last_blk = s // V7X_SUBLANES - 1
    return pl.pallas_call(
        _in_proj_kernel,
        out_shape=[jax.ShapeDtypeStruct((b, s, hi - lo), dt) for (lo, hi), dt in zip(segs, dts)],
        grid=(b, s // tm),
        in_specs=[pl.BlockSpec((1, tm, d), lambda bi, i: (bi, i, 0)),
                  pl.BlockSpec((1, V7X_SUBLANES, d), lambda bi, i: (bi, jnp.maximum(i * per - 1, 0), 0)),
                  pl.BlockSpec((1, V7X_SUBLANES, d), lambda bi, i: (bi, jnp.minimum((i + 1) * per, last_blk), 0)),
                  _mod_block(layer, d, mod_row),
                  _layer_block(layer, 1, d),
                  pl.BlockSpec((None, d, P_PAD), lambda bi, i: (layer, 0, 0), pipeline_mode=pl.Buffered(1)),
                  _layer_block(layer, 3, XBC_W),
                  _layer_block(layer, 1, XBC_W)],
        out_specs=[pl.BlockSpec((1, tm, hi - lo), lambda bi, i: (bi, i, 0)) for lo, hi in segs],
        compiler_params=_params(("parallel", "parallel")),
        name="in_proj",
    )(x, x, x, mods, gain, w_in_p, conv_w, conv_b)


def _q_proj_kernel(cq_ref, an_ref, w_ref, g_ref, g2_ref, cs_ref, q_ref):
    cq = cq_ref[0].astype(F32)
    cqn = _rms(cq, an_ref[...]).astype(BF16)
    gcs = g2_ref[...] * cs_ref[...]
    gc, gs = gcs[:, :V7X_LANES], gcs[:, V7X_LANES:]
    lanes = V7X_LANES
    for pair in range(N_HEADS // 2):
        y = _dot(cqn, w_ref[:, pair * 2 * Q_HEAD_COLS:(pair + 1) * 2 * Q_HEAD_COLS])
        for e in range(2):
            h = 2 * pair + e
            a = y[:, e * lanes:(e + 1) * lanes]
            r = y[:, (2 + e) * lanes:(3 + e) * lanes]
            rp = y[:, (4 + e) * lanes:(5 + e) * lanes]
            ssq = jnp.sum(a * a + r * r, axis=-1, keepdims=True)
            rs = lax.rsqrt(ssq * (1.0 / QK) + NORM_EPS) * (SM_SCALE * LOG2_E)
            q_ref[0, :, h * HEAD_PAD:h * HEAD_PAD + NOPE] = (a * g_ref[...] * rs).astype(BF16)
            q_ref[0, :, h * HEAD_PAD + NOPE:(h + 1) * HEAD_PAD] = ((r * gc + rp * gs) * rs).astype(BF16)


def _q_proj(cq, an, w_uq_p, layer, g, g2, cs, tm):
    b, s, _ = cq.shape
    width = N_HEADS * HEAD_PAD
    return pl.pallas_call(
        _q_proj_kernel,
        out_shape=jax.ShapeDtypeStruct((b, s, width), BF16),
        grid=(b, s // tm),
        in_specs=[pl.BlockSpec((1, tm, Q_LORA), lambda bi, i: (bi, i, 0)),
                  _layer_block(layer, 1, Q_LORA),
                  _layer_block(layer, Q_LORA, N_HEADS * Q_HEAD_COLS),
                  _layer_block(layer, 1, NOPE),
                  _layer_block(layer, 1, 2 * V7X_LANES),
                  pl.BlockSpec((tm, 2 * V7X_LANES), lambda bi, i: (i, 0))],
        out_specs=pl.BlockSpec((1, tm, width), lambda bi, i: (bi, i, 0)),
        compiler_params=_params(("parallel", "parallel")),
        name="q_proj",
    )(cq, an, w_uq_p, g, g2, cs)


def _kv_proj_kernel(t_ref, an_ref, w_ref, g_ref, g2_ref, cs_ref, k_ref, v_ref):
    ckv = t_ref[0, :, :KV_LORA].astype(F32)
    kr = t_ref[0, :, KV_LORA:KV_LORA + V7X_LANES].astype(F32)
    krp = t_ref[0, :, KV_LORA + V7X_LANES:].astype(F32)
    ckvn = _rms(ckv, an_ref[...]).astype(BF16)
    gcs = g2_ref[...] * cs_ref[...]
    rot = kr * gcs[:, :V7X_LANES] + krp * gcs[:, V7X_LANES:]
    kr_sq = kr * kr
    lane = lax.broadcasted_iota(jnp.int32, kr.shape, 1)
    ones_col = jnp.where(lane == 0, 1.0, 0.0).astype(BF16)
    for h in range(N_HEADS):
        y = _dot(ckvn, w_ref[:, h * 2 * NOPE:(h + 1) * 2 * NOPE])
        kn = y[:, :NOPE]
        rs = lax.rsqrt(jnp.sum(kn * kn + kr_sq, axis=-1, keepdims=True) * (1.0 / QK) + NORM_EPS)
        k_ref[0, :, h * HEAD_PAD:h * HEAD_PAD + NOPE] = (kn * g_ref[...] * rs).astype(BF16)
        k_ref[0, :, h * HEAD_PAD + NOPE:(h + 1) * HEAD_PAD] = (rot * rs).astype(BF16)
        v_ref[0, :, h * HEAD_PAD:h * HEAD_PAD + V_DIM] = y[:, NOPE:].astype(BF16)
        v_ref[0, :, h * HEAD_PAD + V_DIM:(h + 1) * HEAD_PAD] = ones_col


def _kv_proj(t, an, w_ukv, layer, g, g2, cs, tm):
    b, s, wt = t.shape
    return pl.pallas_call(
        _kv_proj_kernel,
        out_shape=[jax.ShapeDtypeStruct((b, s, N_HEADS * HEAD_PAD), BF16),
                   jax.ShapeDtypeStruct((b, s, N_HEADS * HEAD_PAD), BF16)],
        grid=(b, s // tm),
        in_specs=[pl.BlockSpec((1, tm, wt), lambda bi, i: (bi, i, 0)),
                  _layer_block(layer, 1, KV_LORA),
                  _layer_block(layer, KV_LORA, N_HEADS * 2 * NOPE),
                  _layer_block(layer, 1, NOPE),
                  _layer_block(layer, 1, 2 * V7X_LANES),
                  pl.BlockSpec((tm, 2 * V7X_LANES), lambda bi, i: (i, 0))],
        out_specs=[pl.BlockSpec((1, tm, N_HEADS * HEAD_PAD), lambda bi, i: (bi, i, 0)),
                   pl.BlockSpec((1, tm, N_HEADS * HEAD_PAD), lambda bi, i: (bi, i, 0))],
        compiler_params=_params(("parallel", "parallel")),
        name="kv_proj",
    )(t, an, w_ukv, g, g2, cs)


def _attn_kernel(*refs, layer, n_lat, tk):
    if n_lat:
        bounded_ref, q_ref, kc_ref, vc_ref, kl_ref, vl_ref, o_ref, s0_ref, s1_ref = refs
    else:
        bounded_ref, q_ref, kc_ref, vc_ref, o_ref = refs
    n_chunks = n_lat // tk if n_lat else 0

    def finish(acc):
        o_ref[0] = (acc[:, :V_DIM] * (1.0 / acc[:, V_DIM:V_DIM + 1])).astype(o_ref.dtype)

    @pl.when(bounded_ref[layer] != 0)
    def _():
        q = q_ref[0]
        acc = _dot(jnp.exp2(_dot_nt(q, kc_ref[0])).astype(BF16), vc_ref[0])
        for j in range(n_chunks):
            p = jnp.exp2(_dot_nt(q, kl_ref[0, j * tk:(j + 1) * tk, :])).astype(BF16)
            acc = acc + _dot(p, vl_ref[0, j * tk:(j + 1) * tk, :])
        finish(acc)

    @pl.when(bounded_ref[layer] == 0)
    def _():
        q = q_ref[0]
        tq = q.shape[0]

        def update(carry, s, v):
            m, acc = carry
            m_new = jnp.maximum(m, jnp.max(s, axis=-1, keepdims=True))
            p = jnp.exp2(s - m_new).astype(BF16)
            return m_new, jnp.exp2(m - m_new) * acc + _dot(p, v)

        carry = (jnp.full((tq, 1), -jnp.inf, F32), jnp.zeros((tq, HEAD_PAD), F32))
        if n_lat:
            slots = (s0_ref, s1_ref)

            def scores(j):
                slots[j % 2][...] = _dot_nt(q, kl_ref[0, j * tk:(j + 1) * tk, :])

            scores(0)
            carry = update(carry, _dot_nt(q, kc_ref[0]), vc_ref[0])
            for j in range(n_chunks):
                if j + 1 < n_chunks:
                    scores(j + 1)
                carry = update(carry, slots[j % 2][...], vl_ref[0, j * tk:(j + 1) * tk, :])
        else:
            carry = update(carry, _dot_nt(q, kc_ref[0]), vc_ref[0])
        finish(carry[1])


def _attention(bounded, layer, q, k_c, v_c, k_l=None, v_l=None, *, tq, tk):
    b, s, _ = q.shape
    n_ctx = k_c.shape[1]
    n_lat = 0 if k_l is None else k_l.shape[1]
    in_specs = [pl.BlockSpec(memory_space=pltpu.SMEM),
                pl.BlockSpec((1, tq, HEAD_PAD), lambda bi, h, i: (bi, i, h)),
                pl.BlockSpec((1, n_ctx, HEAD_PAD), lambda bi, h, i: (bi, 0, h)),
                pl.BlockSpec((1, n_ctx, HEAD_PAD), lambda bi, h, i: (bi, 0, h))]
    args = [bounded, q, k_c, v_c]
    scratch = []
    if n_lat:
        tk = min(tk, n_lat)
        assert n_lat % tk == 0
        in_specs += [pl.BlockSpec((1, n_lat, HEAD_PAD), lambda bi, h, i: (bi, 0, h)),
                     pl.BlockSpec((1, n_lat, HEAD_PAD), lambda bi, h, i: (bi, 0, h))]
        args += [k_l, v_l]
        scratch = [pltpu.VMEM((tq, tk), F32), pltpu.VMEM((tq, tk), F32)]
    return pl.pallas_call(
        functools.partial(_attn_kernel, layer=layer, n_lat=n_lat, tk=tk),
        out_shape=jax.ShapeDtypeStruct((b, s, ATT_W), BF16),
        grid=(b, N_HEADS, s // tq),
        in_specs=in_specs,
        out_specs=pl.BlockSpec((1, tq, V_DIM), lambda bi, h, i: (bi, i, h)),
        scratch_shapes=scratch,
        compiler_params=_params(("parallel", "parallel", "arbitrary")),
        name="attention",
    )(*args)


def _sgu_kernel(p_ref, gn_ref, w_ref, b_ref, o_ref, *, n_chunks):
    for c in range(n_chunks):
        rows = slice(c * SGU_CH, (c + 1) * SGU_CH)
        z = _gelu_tanh(p_ref[0, rows, :].astype(F32))
        u = z[:, :SGU_W]
        vn = _rms(z[:, SGU_W:], gn_ref[...]).astype(BF16)
        for g in range(SGU_G):
            cols = slice(g * SGU_CH, (g + 1) * SGU_CH)
            mixed = _dot(w_ref[g], vn[:, cols]) + b_ref[g]
            o_ref[0, rows, cols] = (u[:, cols] * mixed).astype(o_ref.dtype)


def _sgu(p, gn, w_s, b_b, layer, rows):
    b, s, _ = p.shape
    return pl.pallas_call(
        functools.partial(_sgu_kernel, n_chunks=rows // SGU_CH),
        out_shape=jax.ShapeDtypeStruct((b, s, SGU_W), BF16),
        grid=(b, s // rows),
        in_specs=[pl.BlockSpec((1, rows, 2 * SGU_W), lambda bi, i: (bi, i, 0)),
                  _layer_block(layer, 1, SGU_W),
                  _layer_block(layer, SGU_G, SGU_CH, SGU_CH),
                  _layer_block(layer, SGU_G, SGU_CH, SGU_CH)],
        out_specs=pl.BlockSpec((1, rows, SGU_W), lambda bi, i: (bi, i, 0)),
        compiler_params=_params(("parallel", "parallel")),
        name="sgu",
    )(p, gn, w_s, b_b)


def _ssd_direction(x_ref, dt_ref, bias_ref, alog_ref, tri_ref, exp_ref, state_ref, y_ref, *, bi, reverse):
    q = SSD_Q
    xs = x_ref[bi, :, :SSD_IN].astype(F32)
    dt = _softplus(dt_ref[bi] + bias_ref[...])
    a_dt = dt * (-jnp.exp(alog_ref[...]))
    tri = tri_ref[...]
    acs = sum(_dot(tri, part) for part in _split3(a_dt))
    acs_t = acs.T
    expand = exp_ref[...]
    acs_x = sum(_dot(part, expand) for part in _split3(acs))
    dt_hi = dt.astype(BF16)
    dt_x = _dot(dt_hi, expand) + _dot((dt - dt_hi.astype(F32)).astype(BF16), expand)
    last = acs_x[0:1, :] if reverse else acs_x[q - 1:q, :]
    xdt = xs * dt_x
    x_end = (xdt * jnp.exp(last - acs_x)).astype(BF16)
    decay_out = jnp.exp(acs_x)
    chunk_decay = jnp.exp(last)

    ti = lax.broadcasted_iota(jnp.int32, (q, q), 0)
    tj = lax.broadcasted_iota(jnp.int32, (q, q), 1)
    keep = (ti <= tj) if reverse else (ti >= tj)
    lane_blk = lax.shift_right_logical(lax.broadcasted_iota(jnp.int32, (q, GRP_W), 1), int(math.log2(SSD_P)))
    head0 = SSD_H if reverse else 0
    heads_per_group = SSD_H // SSD_G
    for g in range(SSD_G):
        bm = x_ref[bi, :, SSD_IN + g * SSD_N:SSD_IN + (g + 1) * SSD_N]
        cm = x_ref[bi, :, SSD_IN + (SSD_G + g) * SSD_N:SSD_IN + (SSD_G + g + 1) * SSD_N]
        cb = _dot_nt(cm, bm)
        cols = slice(g * GRP_W, (g + 1) * GRP_W)
        state = state_ref[g]
        y_off = _dot(cm, state.astype(BF16)) * decay_out[:, cols]
        state_ref[g] = state * chunk_decay[:, cols] + _dot(bm.astype(F32).T.astype(BF16), x_end[:, cols])
        blocks, stacked = [], []
        for hh in range(heads_per_group):
            hcol = head0 + g * heads_per_group + hh
            seg = acs[:, hcol:hcol + 1] - acs_t[hcol:hcol + 1, :]
            blocks.append((cb * jnp.exp(jnp.where(keep, seg, -jnp.inf))).astype(BF16))
            stacked.append(jnp.where(lane_blk == hh, xdt[:, cols], 0.0).astype(BF16))
        y = y_off + _dot(jnp.concatenate(blocks, axis=1), jnp.concatenate(stacked, axis=0))
        y_ref[bi, :, cols] = y.astype(y_ref.dtype)


def _ssd_kernel(xf_ref, dtf_ref, xb_ref, dtb_ref, bias_ref, alog_ref, tri_ref, exp_ref, init_ref,
                yf_ref, yb_ref, fin_ref, state_ref):
    s = pl.program_id(0)

    @pl.when(s == 0)
    def _():
        state_ref[...] = init_ref[...]

    for bi in range(xf_ref.shape[0]):
        _ssd_direction(xf_ref, dtf_ref, bias_ref, alog_ref, tri_ref.at[0], exp_ref.at[0], state_ref.at[bi, 0],
                       yf_ref, bi=bi, reverse=False)
        _ssd_direction(xb_ref, dtb_ref, bias_ref, alog_ref, tri_ref.at[1], exp_ref.at[1], state_ref.at[bi, 1],
                       yb_ref, bi=bi, reverse=True)

    @pl.when(s == pl.num_programs(0) - 1)
    def _():
        fin_ref[...] = state_ref[...]


def _ssd(xbc, dt, bias, alog, tri, expand, init, layer):
    b, s, _ = xbc.shape
    nc = s // SSD_Q

    def main_f(i): return (0, i, 0)
    def main_b(i): return (0, nc - 1 - i, 0)
    const3 = lambda i: (0, 0, 0)
    state_shape = (b, 2, SSD_G, SSD_N, GRP_W)
    state_block = pl.BlockSpec(state_shape, lambda i: (0, 0, 0, 0, 0))
    return pl.pallas_call(
        _ssd_kernel,
        out_shape=[jax.ShapeDtypeStruct((b, s, SSD_IN), BF16), jax.ShapeDtypeStruct((b, s, SSD_IN), BF16),
                   jax.ShapeDtypeStruct(state_shape, F32)],
        grid=(nc,),
        in_specs=[pl.BlockSpec((b, SSD_Q, XBC_W), main_f), pl.BlockSpec((b, SSD_Q, V7X_LANES), main_f),
                  pl.BlockSpec((b, SSD_Q, XBC_W), main_b), pl.BlockSpec((b, SSD_Q, V7X_LANES), main_b),
                  _layer_block(layer, 1, V7X_LANES), _layer_block(layer, 1, V7X_LANES),
                  pl.BlockSpec((2, SSD_Q, SSD_Q), const3), pl.BlockSpec((2, V7X_LANES, SSD_IN), const3),
                  state_block],
        out_specs=[pl.BlockSpec((b, SSD_Q, SSD_IN), main_f), pl.BlockSpec((b, SSD_Q, SSD_IN), main_b),
                   state_block],
        scratch_shapes=[pltpu.VMEM(state_shape, F32)],
        compiler_params=_params(("arbitrary",)),
        name="ssd",
    )(xbc, dt, xbc, dt, bias, alog, tri, expand, init)


def _out_proj_kernel(x_ref, mod_ref, att_ref, sgu_ref, yf_ref, yb_ref, xs_ref, z_ref,
                     ga_ref, gs_ref, d_ref, gy_ref, gf_ref, w_ref, o_ref, h_ref):
    att = _rms(att_ref[0].astype(F32), ga_ref[...]).astype(BF16)
    sgu = _rms(sgu_ref[0].astype(F32), gs_ref[...]).astype(BF16)
    y = yf_ref[0].astype(F32) + yb_ref[0].astype(F32) + d_ref[...] * xs_ref[0].astype(F32)
    ssd = _rms(y * _silu(z_ref[0].astype(F32)), gy_ref[...]).astype(BF16)
    mix = (_dot(att, w_ref[:ATT_W, :]) + _dot(sgu, w_ref[ATT_W:ATT_W + SGU_W, :])
           + _dot(ssd, w_ref[ATT_W + SGU_W:, :]))
    x_new = x_ref[0] + mod_ref[0, 2:3, :] * mix
    o_ref[0] = x_new
    h_ref[0] = (_rms(x_new, gf_ref[...]) * (1.0 + mod_ref[0, 4:5, :]) + mod_ref[0, 3:4, :]).astype(BF16)


def _out_proj(x, mods, mod_row, att, sgu, yf, yb, xs, z, ga, gs, dvec, gy, gf, w_out, layer, tm):
    b, s, d = x.shape
    row = lambda w: pl.BlockSpec((1, tm, w), lambda bi, i: (bi, i, 0))
    vec = lambda w: _layer_block(layer, 1, w)
    return pl.pallas_call(
        _out_proj_kernel,
        out_shape=[jax.ShapeDtypeStruct((b, s, d), F32), jax.ShapeDtypeStruct((b, s, d), BF16)],
        grid=(b, s // tm),
        in_specs=[row(d), _mod_block(layer, d, mod_row),
                  row(ATT_W), row(SGU_W), row(SSD_IN), row(SSD_IN), row(SSD_IN), row(SSD_IN),
                  vec(ATT_W), vec(SGU_W), vec(SSD_IN), vec(SSD_IN), vec(d),
                  pl.BlockSpec((None,) + w_out.shape[1:], lambda bi, i: (layer, 0, 0),
                               pipeline_mode=pl.Buffered(1))],
        out_specs=[row(d), row(d)],
        compiler_params=_params(("parallel", "parallel")),
        name="out_proj",
    )(x, mods, att, sgu, yf, yb, xs, z, ga, gs, dvec, gy, gf, w_out)


def _ffn_kernel(x_ref, h_ref, hp_ref, hn_ref, mod_ref, wg_ref, wu_ref, cw_ref, cb_ref, wd_ref, o_ref,
                halo_sc, acc_sc, *, seq_rows):
    i = pl.program_id(1)
    j = pl.program_id(2)
    hb = BF16_ROWS

    @pl.when(j == 0)
    def _():
        halo_sc[0:hb, :] = jnp.where(i > 0, hp_ref[0], jnp.zeros_like(hp_ref[0]))
        halo_sc[hb:, :] = jnp.where(i < pl.num_programs(1) - 1, hn_ref[0], jnp.zeros_like(hn_ref[0]))
        acc_sc[...] = jnp.zeros_like(acc_sc)

    h = h_ref[0]
    tm = h.shape[0]
    gate = _dot(h, wg_ref[...])
    gate_halo = _dot(halo_sc[...], wg_ref[...])
    row = lax.broadcasted_iota(jnp.int32, gate.shape, 0)
    up = jnp.where(row == 0, gate_halo[hb - 1:hb, :], pltpu.roll(gate, 1, axis=0))
    dn = jnp.where(row == tm - 1, gate_halo[hb:hb + 1, :], pltpu.roll(gate, tm - 1, axis=0))
    if seq_rows is not None:
        pos = lax.rem(row, seq_rows)
        up = jnp.where(pos == 0, 0.0, up)
        dn = jnp.where(pos == seq_rows - 1, 0.0, dn)
    conv = cb_ref[...] + cw_ref[0:1, :] * up + cw_ref[1:2, :] * gate + cw_ref[2:3, :] * dn
    act = (_silu(conv) * _dot(h, wu_ref[...])).astype(BF16)
    acc_sc[...] += _dot(act, wd_ref[...])

    @pl.when(j == pl.num_programs(2) - 1)
    def _():
        o_ref[0] = x_ref[0] + mod_ref[0, 5:6, :] * acc_sc[...]


def _ffn(x, h, mods, mod_row, w_gate, w_up, conv_w, conv_b, w_down, layer, tm, tf, pack_sequences=False):
    out_shape = x.shape
    seq_rows = None
    if pack_sequences:
        assert mod_row is not None
        seq_rows, d = x.shape[1], x.shape[2]
        x, h = x.reshape(1, -1, d), h.reshape(1, -1, d)
        tm = x.shape[1]
    b, s, d = x.shape
    f = w_down.shape[1]
    per = tm // BF16_ROWS
    last_blk = s // BF16_ROWS - 1
    assert f % tf == 0
    return pl.pallas_call(
        functools.partial(_ffn_kernel, seq_rows=seq_rows),
        out_shape=jax.ShapeDtypeStruct((b, s, d), F32),
        grid=(b, s // tm, f // tf),
        in_specs=[pl.BlockSpec((1, tm, d), lambda bi, i, j: (bi, i, 0)),
                  pl.BlockSpec((1, tm, d), lambda bi, i, j: (bi, i, 0)),
                  pl.BlockSpec((1, BF16_ROWS, d), lambda bi, i, j: (bi, jnp.maximum(i * per - 1, 0), 0)),
                  pl.BlockSpec((1, BF16_ROWS, d), lambda bi, i, j: (bi, jnp.minimum((i + 1) * per, last_blk), 0)),
                  _mod_block(layer, d, mod_row),
                  pl.BlockSpec((None, d, tf), lambda bi, i, j: (layer, 0, j)),
                  pl.BlockSpec((None, d, tf), lambda bi, i, j: (layer, 0, j)),
                  pl.BlockSpec((None, 3, tf), lambda bi, i, j: (layer, 0, j)),
                  pl.BlockSpec((None, 1, tf), lambda bi, i, j: (layer, 0, j)),
                  pl.BlockSpec((None, tf, d), lambda bi, i, j: (layer, j, 0))],
        out_specs=pl.BlockSpec((1, tm, d), lambda bi, i, j: (bi, i, 0)),
        scratch_shapes=[pltpu.VMEM((2 * BF16_ROWS, d), BF16), pltpu.VMEM((tm, d), F32)],
        compiler_params=_params(("parallel", "parallel", "arbitrary")),
        name="ffn",
    )(x, h, h, h, mods, w_gate, w_up, conv_w, conv_b, w_down).reshape(out_shape)


def _rope_tables(seq):
    rows = seq // GRID_COLS
    pairs = ROPE // 4
    freqs = ROPE_THETA ** (-jnp.arange(pairs, dtype=F32) / pairs)
    ar = jnp.arange(rows, dtype=F32)[:, None] * freqs
    ac = jnp.arange(GRID_COLS, dtype=F32)[:, None] * freqs
    by_row = lambda t: jnp.repeat(t, GRID_COLS, axis=0)
    by_col = lambda t: jnp.tile(t, (rows, 1))
    cr, sr, cc, sc = by_row(jnp.cos(ar)), by_row(jnp.sin(ar)), by_col(jnp.cos(ac)), by_col(jnp.sin(ac))
    cos = jnp.concatenate([cr, cr, cc, cc], axis=-1)
    sin = jnp.concatenate([-sr, sr, -sc, sc], axis=-1)
    zeros = jnp.zeros_like(cos)
    return jnp.concatenate([cos, zeros, sin, zeros], axis=-1)


def _pack_w_in(w):
    w = w.astype(BF16)
    off_kr, off_sgu = Q_LORA + KV_LORA, Q_LORA + KV_LORA + ROPE
    off_dt = off_sgu + 2 * SGU_W + SSD_IN + XBC_W
    kr = w[..., off_kr:off_sgu]
    dt = w[..., off_dt:]
    z64 = jnp.zeros(w.shape[:2] + (V7X_LANES - ROPE,), BF16)
    return jnp.concatenate([w[..., :off_kr], kr, z64, kr[..., _ROT_PERM], z64, w[..., off_sgu:off_dt], dt,
                            jnp.zeros(w.shape[:2] + (V7X_LANES - dt.shape[-1],), BF16)], axis=-1)


def _pack_w_uq(w):
    depth = w.shape[0]
    pairs = N_HEADS // 2
    w = w.astype(BF16).reshape(depth, Q_LORA, pairs, 2, QK)
    rot = w[..., NOPE:]
    z64 = jnp.zeros((depth, Q_LORA, pairs, 2, V7X_LANES - ROPE), BF16)
    both = lambda t: t.reshape(depth, Q_LORA, pairs, 2 * V7X_LANES)
    return jnp.concatenate([both(w[..., :NOPE]), both(jnp.concatenate([rot, z64], axis=-1)),
                            both(jnp.concatenate([rot[..., _ROT_PERM], z64], axis=-1))],
                           axis=-1).reshape(depth, Q_LORA, N_HEADS * Q_HEAD_COLS)


def _rot_gains(g):
    rot = g[:, NOPE:]
    z64 = jnp.zeros((g.shape[0], V7X_LANES - ROPE), g.dtype)
    return jnp.concatenate([rot, z64, rot[:, _ROT_PERM], z64], axis=-1)[:, None, :]


def _row_vectors(v, width=None):
    v = v.reshape(v.shape[0], 1, -1)
    return v if width is None else jnp.pad(v, ((0, 0), (0, 0), (0, width - v.shape[-1])))


def kernel(x, c, ctx, c_ctx, w_mod, b_mod, norm_mix, norm_ffn, w_in, q_a_norm, w_uq, kv_a_norm, w_ukv, q_norm, k_norm, attn_out_norm, sgu_norm, sgu_w, sgu_b, gmlp_out_norm, ssd_conv_w, ssd_conv_b, ssd_dt_bias, ssd_a_log, ssd_d, ssd_norm, w_out, ffn_w_gate, ffn_w_up, ffn_conv_w, ffn_conv_b, ffn_w_down):
    batch, seq, d = x.shape
    n_ctx = ctx.shape[1]
    depth = w_mod.shape[0]
    tm = min(512, seq)
    tm_ctx = min(256, n_ctx)
    tf = 512
    tk_att = 1024
    tq_att = 1024

    cc = jnp.concatenate([c, c_ctx[None], jnp.zeros((V7X_SUBLANES - batch - 1, d), F32)], axis=0)
    mods = jnp.pad(_mods(cc, w_mod, b_mod).reshape(depth, V7X_SUBLANES, 6, d),
                   ((0, 0), (0, 0), (0, V7X_SUBLANES - 6), (0, 0)))
    ctx_row = batch

    cs_lat = _rope_tables(seq)
    cs_ctx = jnp.concatenate([jnp.ones((n_ctx, ROPE), F32), jnp.zeros((n_ctx, 2 * V7X_LANES - ROPE), F32)], axis=-1)

    ti = np.arange(SSD_Q)
    tri = jnp.asarray(np.stack([ti[:, None] >= ti[None, :], ti[:, None] <= ti[None, :]]), BF16)
    lane_head = np.arange(SSD_IN) // SSD_P
    col = np.arange(V7X_LANES)
    expand = jnp.asarray(np.stack([col[:, None] == lane_head[None, :],
                                   col[:, None] == lane_head[None, :] + SSD_H]), BF16)

    w_in_p, w_uq_p = _pack_w_in(w_in), _pack_w_uq(w_uq)
    w_ukv_b, w_out_b = w_ukv.astype(BF16), w_out.astype(BF16)
    wg, wu, wd = ffn_w_gate.astype(BF16), ffn_w_up.astype(BF16), ffn_w_down.astype(BF16)

    g_mix, g_ffn = _row_vectors(norm_mix), _row_vectors(norm_ffn)
    an_q, an_kv = _row_vectors(q_a_norm), _row_vectors(kv_a_norm)
    gq1, gk1 = _row_vectors(q_norm[:, :NOPE]), _row_vectors(k_norm[:, :NOPE])
    gq2, gk2 = _rot_gains(q_norm), _rot_gains(k_norm)
    g_att, g_sgu_in, g_sgu_out = _row_vectors(attn_out_norm), _row_vectors(sgu_norm), _row_vectors(gmlp_out_norm)
    g_ssd = _row_vectors(ssd_norm)
    sgu_w_b = sgu_w.astype(BF16)
    sgu_b_b = jnp.broadcast_to(sgu_b[..., None], sgu_b.shape + (SGU_CH,))
    conv_b = _row_vectors(ssd_conv_b)
    bias, alog = _row_vectors(ssd_dt_bias, V7X_LANES), _row_vectors(ssd_a_log, V7X_LANES)
    dvec = _row_vectors(jnp.repeat(ssd_d, SSD_P, axis=1))
    fcb = _row_vectors(ffn_conv_b)
    score_bound = (QK * SM_SCALE * LOG2_E) * jnp.max(jnp.abs(q_norm), axis=1) * jnp.max(jnp.abs(k_norm), axis=1)
    bounded = (score_bound <= SCORE_BOUND_LIMIT).astype(jnp.int32)

    x_lat, x_ctx = x, ctx
    for l in range(depth):
        need_ctx = l < depth - 1

        def mixers(xs_in, mod_row, cs, t_rows, init_state, k_c=None, v_c=None, outputs=True):
            pq, pkv, psgu, pz, xbc, pdt = _in_proj(xs_in, mods, mod_row, g_mix, w_in_p, ssd_conv_w, conv_b, l, t_rows)
            k, v = _kv_proj(pkv, an_kv, w_ukv_b, l, gk1, gk2, cs, t_rows)
            yf, yb, fin = _ssd(xbc, pdt, bias, alog, tri, expand, init_state, l)
            if not outputs:
                return None, k, v, fin
            q = _q_proj(pq, an_q, w_uq_p, l, gq1, gq2, cs, t_rows)
            if k_c is None:
                att = _attention(bounded, l, q, k, v, tq=t_rows, tk=tk_att)
            else:
                att = _attention(bounded, l, q, k_c, v_c, k, v, tq=min(tq_att, q.shape[1]), tk=tk_att)
            sgu = _sgu(psgu, g_sgu_in, sgu_w_b, sgu_b_b, l, t_rows)
            out = _out_proj(xs_in, mods, mod_row, att, sgu, yf, yb, xbc, pz, g_att, g_sgu_out, dvec, g_ssd, g_ffn,
                            w_out_b, l, t_rows)
            return out, k, v, fin

        zero_state = jnp.zeros((batch, 2, SSD_G, SSD_N, GRP_W), F32)
        ctx_mid, k_c, v_c, ctx_state = mixers(x_ctx, ctx_row, cs_ctx, tm_ctx, zero_state, outputs=need_ctx)
        (x_lat, h_lat), _, _, _ = mixers(x_lat, None, cs_lat, tm, ctx_state, k_c, v_c)
        x_lat = _ffn(x_lat, h_lat, mods, None, wg, wu, ffn_conv_w, fcb, wd, l, tm, tf)
        if need_ctx:
            x_ctx = _ffn(ctx_mid[0], ctx_mid[1], mods, ctx_row, wg, wu, ffn_conv_w, fcb, wd, l, tm_ctx, tf,
                         pack_sequences=True)
    return x_lat
```

```python
import functools
import math

import jax
import jax.numpy as jnp
import numpy as np
from jax import lax
from jax.experimental import pallas as pl
from jax.experimental.pallas import tpu as pltpu

F32 = jnp.float32
BF16 = jnp.bfloat16
NORM_EPS = 1e-6

GRID_COLS = 64
N_HEADS = 8
NOPE = 128
ROPE = 64
QK = NOPE + ROPE
V_DIM = 128
Q_LORA = 768
KV_LORA = 512
ROPE_THETA = 10000.0
SM_SCALE = 1.0 / math.sqrt(QK)
LOG2_E = math.log2(math.e)
SCORE_BOUND_LIMIT = 60.0
HEAD_PAD = 256
ATT_W = N_HEADS * V_DIM
SGU_G = 4
SGU_CH = 128
SGU_W = 512
SSD_H = 8
SSD_P = 64
SSD_IN = SSD_H * SSD_P
SSD_N = 128
SSD_G = 2
SSD_Q = 128
XBC_W = SSD_IN + 2 * SSD_G * SSD_N
GRP_W = (SSD_H // SSD_G) * SSD_P

V7X_LANES = 128
V7X_SUBLANES = 8
BF16_ROWS = 2 * V7X_SUBLANES
V7X_VMEM_BYTES = 64 * 1024 * 1024
VMEM_LIMIT = V7X_VMEM_BYTES - 8 * 1024 * 1024

SEG_Q = (0, Q_LORA)
SEG_KV = (SEG_Q[1], SEG_Q[1] + KV_LORA + 2 * V7X_LANES)
SEG_SGU = (SEG_KV[1], SEG_KV[1] + 2 * SGU_W)
SEG_Z = (SEG_SGU[1], SEG_SGU[1] + SSD_IN)
SEG_XBC = (SEG_Z[1], SEG_Z[1] + XBC_W)
SEG_DT = (SEG_XBC[1], SEG_XBC[1] + V7X_LANES)
P_PAD = SEG_DT[1]
Q_HEAD_COLS = NOPE + 2 * V7X_LANES

_ROT_PERM = np.concatenate([np.arange(16, 32), np.arange(0, 16), np.arange(48, 64), np.arange(32, 48)])


def _params(sem, vmem=VMEM_LIMIT):
    return pltpu.CompilerParams(dimension_semantics=sem, vmem_limit_bytes=vmem)


def _layer_block(layer, *shape):
    zeros = (0,) * len(shape)
    return pl.BlockSpec((None,) + shape, lambda *_: (layer,) + zeros)


def _mod_block(layer, d, fixed_row):
    if fixed_row is None:
        return pl.BlockSpec((None, 1, V7X_SUBLANES, d), lambda bi, *_: (layer, bi, 0, 0))
    return pl.BlockSpec((None, 1, V7X_SUBLANES, d), lambda *_: (layer, fixed_row, 0, 0))


def _dot(a, b):
    return jnp.dot(a, b, preferred_element_type=F32)


def _dot_nt(a, b):
    return lax.dot_general(a, b, (((1,), (1,)), ((), ())), preferred_element_type=F32)


def _rms(t, gain):
    return t * lax.rsqrt(jnp.mean(t * t, axis=-1, keepdims=True) + NORM_EPS) * gain


def _silu(t):
    return t * (1.0 / (1.0 + jnp.exp(-t)))


def _gelu_tanh(t):
    return 0.5 * t * (1.0 + jnp.tanh(math.sqrt(2.0 / math.pi) * (t + 0.044715 * (t * t * t))))


def _softplus(t):
    return jnp.maximum(t, 0.0) + jnp.log1p(jnp.exp(-jnp.abs(t)))


def _split3(t):
    hi = t.astype(BF16)
    r1 = t - hi.astype(F32)
    mid = r1.astype(BF16)
    lo = (r1 - mid.astype(F32)).astype(BF16)
    return hi, mid, lo


def _mods_kernel(c_ref, w_ref, b_ref, o_ref):
    s = _silu(c_ref[...])
    s_hi = s.astype(BF16)
    s_lo = (s - s_hi.astype(F32)).astype(BF16)
    w = w_ref[0]
    w_hi = w.astype(BF16)
    w_lo = (w - w_hi.astype(F32)).astype(BF16)
    o_ref[0] = _dot(s_hi, w_hi) + _dot(s_lo, w_hi) + _dot(s_hi, w_lo) + b_ref[0]


def _mods(cc, w_mod, b_mod):
    depth, d, n6 = w_mod.shape
    tn = 1024
    return pl.pallas_call(
        _mods_kernel,
        out_shape=jax.ShapeDtypeStruct((depth, V7X_SUBLANES, n6), F32),
        grid=(depth, n6 // tn),
        in_specs=[pl.BlockSpec((V7X_SUBLANES, d), lambda l, j: (0, 0)),
                  pl.BlockSpec((1, d, tn), lambda l, j: (l, 0, j)),
                  pl.BlockSpec((1, 1, tn), lambda l, j: (l, 0, j))],
        out_specs=pl.BlockSpec((1, V7X_SUBLANES, tn), lambda l, j: (l, 0, j)),
        compiler_params=_params(("parallel", "parallel")),
        name="mods",
    )(cc, w_mod, b_mod.reshape(depth, 1, n6))


def _in_proj_kernel(x_ref, xp_ref, xn_ref, mod_ref, g_ref, w_ref, cw_ref, cb_ref,
                    q_ref, kv_ref, sgu_ref, z_ref, xbc_ref, dt_ref):
    i = pl.program_id(1)
    shift = mod_ref[0, 0:1, :]
    scale = mod_ref[0, 1:2, :]

    def norm_mod(t):
        return _rms(t, g_ref[...]) * (1.0 + scale) + shift

    a, b = SEG_XBC
    halo = jnp.concatenate([norm_mod(xp_ref[0]), norm_mod(xn_ref[0])], axis=0).astype(BF16)
    pre_halo = _dot(halo, w_ref[:, a:b])
    prev_row = jnp.where(i > 0, pre_halo[V7X_SUBLANES - 1:V7X_SUBLANES, :], 0.0)
    next_row = jnp.where(i < pl.num_programs(1) - 1, pre_halo[V7X_SUBLANES:V7X_SUBLANES + 1, :], 0.0)

    h = norm_mod(x_ref[0]).astype(BF16)
    tm = h.shape[0]
    pre = _dot(h, w_ref[:, a:b])
    row = lax.broadcasted_iota(jnp.int32, pre.shape, 0)
    up = jnp.where(row == 0, prev_row, pltpu.roll(pre, 1, axis=0))
    dn = jnp.where(row == tm - 1, next_row, pltpu.roll(pre, tm - 1, axis=0))
    conv = cb_ref[...] + cw_ref[0:1, :] * up + cw_ref[1:2, :] * pre + cw_ref[2:3, :] * dn
    xbc_ref[0] = _silu(conv).astype(xbc_ref.dtype)

    for ref, (a, b) in ((q_ref, SEG_Q), (kv_ref, SEG_KV), (sgu_ref, SEG_SGU), (z_ref, SEG_Z), (dt_ref, SEG_DT)):
        ref[0] = _dot(h, w_ref[:, a:b]).astype(ref.dtype)


def _in_proj(x, mods, mod_row, gain, w_in_p, conv_w, conv_b, layer, tm):
    b, s, d = x.shape
    segs = (SEG_Q, SEG_KV, SEG_SGU, SEG_Z, SEG_XBC, SEG_DT)
    dts = (BF16, BF16, BF16, BF16, BF16, F32)
    per = tm // V7X_SUBLANES
    last_blk = s // V7X_SUBLANES - 1
    return pl.pallas_call(
        _in_proj_kernel,
        out_shape=[jax.ShapeDtypeStruct((b, s, hi - lo), dt) for (lo, hi), dt in zip(segs, dts)],
        grid=(b, s // tm),
        in_specs=[pl.BlockSpec((1, tm, d), lambda bi, i: (bi, i, 0)),
                  pl.BlockSpec((1, V7X_SUBLANES, d), lambda bi, i: (bi, jnp.maximum(i * per - 1, 0), 0)),
                  pl.BlockSpec((1, V7X_SUBLANES, d), lambda bi, i: (bi, jnp.minimum((i + 1) * per, last_blk), 0)),
                  _mod_block(layer, d, mod_row),
                  _layer_block(layer, 1, d),
                  pl.BlockSpec((None, d, P_PAD), lambda bi, i: (layer, 0, 0), pipeline_mode=pl.Buffered(1)),
                  _layer_block(layer, 3, XBC_W),
                  _layer_block(layer, 1, XBC_W)],
        out_specs=[pl.BlockSpec((1, tm, hi - lo), lambda bi, i: (bi, i, 0)) for lo, hi in segs],
        compiler_params=_params(("parallel", "parallel")),
        name="in_proj",
    )(x, x, x, mods, gain, w_in_p, conv_w, conv_b)


def _q_proj_kernel(cq_ref, an_ref, w_ref, g_ref, g2_ref, cs_ref, q_ref):
    cq = cq_ref[0].astype(F32)
    cqn = _rms(cq, an_ref[...]).astype(BF16)
    gcs = g2_ref[...] * cs_ref[...]
    gc, gs = gcs[:, :V7X_LANES], gcs[:, V7X_LANES:]
    lanes = V7X_LANES
    for pair in range(N_HEADS // 2):
        y = _dot(cqn, w_ref[:, pair * 2 * Q_HEAD_COLS:(pair + 1) * 2 * Q_HEAD_COLS])
        for e in range(2):
            h = 2 * pair + e
            a = y[:, e * lanes:(e + 1) * lanes]
            r = y[:, (2 + e) * lanes:(3 + e) * lanes]
            rp = y[:, (4 + e) * lanes:(5 + e) * lanes]
            ssq = jnp.sum(a * a + r * r, axis=-1, keepdims=True)
            rs = lax.rsqrt(ssq * (1.0 / QK) + NORM_EPS) * (SM_SCALE * LOG2_E)
            q_ref[0, :, h * HEAD_PAD:h * HEAD_PAD + NOPE] = (a * g_ref[...] * rs).astype(BF16)
            q_ref[0, :, h * HEAD_PAD + NOPE:(h + 1) * HEAD_PAD] = ((r * gc + rp * gs) * rs).astype(BF16)


def _q_proj(cq, an, w_uq_p, layer, g, g2, cs, tm):
    b, s, _ = cq.shape
    width = N_HEADS * HEAD_PAD
    return pl.pallas_call(
        _q_proj_kernel,
        out_shape=jax.ShapeDtypeStruct((b, s, width), BF16),
        grid=(b, s // tm),
        in_specs=[pl.BlockSpec((1, tm, Q_LORA), lambda bi, i: (bi, i, 0)),
                  _layer_block(layer, 1, Q_LORA),
                  _layer_block(layer, Q_LORA, N_HEADS * Q_HEAD_COLS),
                  _layer_block(layer, 1, NOPE),
                  _layer_block(layer, 1, 2 * V7X_LANES),
                  pl.BlockSpec((tm, 2 * V7X_LANES), lambda bi, i: (i, 0))],
        out_specs=pl.BlockSpec((1, tm, width), lambda bi, i: (bi, i, 0)),
        compiler_params=_params(("parallel", "parallel")),
        name="q_proj",
    )(cq, an, w_uq_p, g, g2, cs)


def _kv_proj_kernel(t_ref, an_ref, w_ref, g_ref, g2_ref, cs_ref, k_ref, v_ref):
    ckv = t_ref[0, :, :KV_LORA].astype(F32)
    kr = t_ref[0, :, KV_LORA:KV_LORA + V7X_LANES].astype(F32)
    krp = t_ref[0, :, KV_LORA + V7X_LANES:].astype(F32)
    ckvn = _rms(ckv, an_ref[...]).astype(BF16)
    gcs = g2_ref[...] * cs_ref[...]
    rot = kr * gcs[:, :V7X_LANES] + krp * gcs[:, V7X_LANES:]
    kr_sq = kr * kr
    lane = lax.broadcasted_iota(jnp.int32, kr.shape, 1)
    ones_col = jnp.where(lane == 0, 1.0, 0.0).astype(BF16)
    for h in range(N_HEADS):
        y = _dot(ckvn, w_ref[:, h * 2 * NOPE:(h + 1) * 2 * NOPE])
        kn = y[:, :NOPE]
        rs = lax.rsqrt(jnp.sum(kn * kn + kr_sq, axis=-1, keepdims=True) * (1.0 / QK) + NORM_EPS)
        k_ref[0, :, h * HEAD_PAD:h * HEAD_PAD + NOPE] = (kn * g_ref[...] * rs).astype(BF16)
        k_ref[0, :, h * HEAD_PAD + NOPE:(h + 1) * HEAD_PAD] = (rot * rs).astype(BF16)
        v_ref[0, :, h * HEAD_PAD:h * HEAD_PAD + V_DIM] = y[:, NOPE:].astype(BF16)
        v_ref[0, :, h * HEAD_PAD + V_DIM:(h + 1) * HEAD_PAD] = ones_col


def _kv_proj(t, an, w_ukv, layer, g, g2, cs, tm):
    b, s, wt = t.shape
    return pl.pallas_call(
        _kv_proj_kernel,
        out_shape=[jax.ShapeDtypeStruct((b, s, N_HEADS * HEAD_PAD), BF16),
                   jax.ShapeDtypeStruct((b, s, N_HEADS * HEAD_PAD), BF16)],
        grid=(b, s // tm),
        in_specs=[pl.BlockSpec((1, tm, wt), lambda bi, i: (bi, i, 0)),
                  _layer_block(layer, 1, KV_LORA),
                  _layer_block(layer, KV_LORA, N_HEADS * 2 * NOPE),
                  _layer_block(layer, 1, NOPE),
                  _layer_block(layer, 1, 2 * V7X_LANES),
                  pl.BlockSpec((tm, 2 * V7X_LANES), lambda bi, i: (i, 0))],
        out_specs=[pl.BlockSpec((1, tm, N_HEADS * HEAD_PAD), lambda bi, i: (bi, i, 0)),
                   pl.BlockSpec((1, tm, N_HEADS * HEAD_PAD), lambda bi, i: (bi, i, 0))],
        compiler_params=_params(("parallel", "parallel")),
        name="kv_proj",
    )(t, an, w_ukv, g, g2, cs)


def _attn_kernel(*refs, layer, n_lat, tk):
    if n_lat:
        bounded_ref, q_ref, kc_ref, vc_ref, kl_ref, vl_ref, o_ref, s0_ref, s1_ref = refs
    else:
        bounded_ref, q_ref, kc_ref, vc_ref, o_ref = refs
    n_chunks = n_lat // tk if n_lat else 0

    def finish(acc):
        o_ref[0] = (acc[:, :V_DIM] * (1.0 / acc[:, V_DIM:V_DIM + 1])).astype(o_ref.dtype)

    @pl.when(bounded_ref[layer] != 0)
    def _():
        q = q_ref[0]
        acc = _dot(jnp.exp2(_dot_nt(q, kc_ref[0])).astype(BF16), vc_ref[0])
        for j in range(n_chunks):
            p = jnp.exp2(_dot_nt(q, kl_ref[0, j * tk:(j + 1) * tk, :])).astype(BF16)
            acc = acc + _dot(p, vl_ref[0, j * tk:(j + 1) * tk, :])
        finish(acc)

    @pl.when(bounded_ref[layer] == 0)
    def _():
        q = q_ref[0]
        tq = q.shape[0]

        def update(carry, s, v):
            m, acc = carry
            m_new = jnp.maximum(m, jnp.max(s, axis=-1, keepdims=True))
            p = jnp.exp2(s - m_new).astype(BF16)
            return m_new, jnp.exp2(m - m_new) * acc + _dot(p, v)

        carry = (jnp.full((tq, 1), -jnp.inf, F32), jnp.zeros((tq, HEAD_PAD), F32))
        if n_lat:
            slots = (s0_ref, s1_ref)

            def scores(j):
                slots[j % 2][...] = _dot_nt(q, kl_ref[0, j * tk:(j + 1) * tk, :])

            scores(0)
            carry = update(carry, _dot_nt(q, kc_ref[0]), vc_ref[0])
            for j in range(n_chunks):
                if j + 1 < n_chunks:
                    scores(j + 1)
                carry = update(carry, slots[j % 2][...], vl_ref[0, j * tk:(j + 1) * tk, :])
        else:
            carry = update(carry, _dot_nt(q, kc_ref[0]), vc_ref[0])
        finish(carry[1])


def _attention(bounded, layer, q, k_c, v_c, k_l=None, v_l=None, *, tq, tk):
    b, s, _ = q.shape
    n_ctx = k_c.shape[1]
    n_lat = 0 if k_l is None else k_l.shape[1]
    in_specs = [pl.BlockSpec(memory_space=pltpu.SMEM),
                pl.BlockSpec((1, tq, HEAD_PAD), lambda bi, h, i: (bi, i, h)),
                pl.BlockSpec((1, n_ctx, HEAD_PAD), lambda bi, h, i: (bi, 0, h)),
                pl.BlockSpec((1, n_ctx, HEAD_PAD), lambda bi, h, i: (bi, 0, h))]
    args = [bounded, q, k_c, v_c]
    scratch = []
    if n_lat:
        tk = min(tk, n_lat)
        assert n_lat % tk == 0
        in_specs += [pl.BlockSpec((1, n_lat, HEAD_PAD), lambda bi, h, i: (bi, 0, h)),
                     pl.BlockSpec((1, n_lat, HEAD_PAD), lambda bi, h, i: (bi, 0, h))]
        args += [k_l, v_l]
        scratch = [pltpu.VMEM((tq, tk), F32), pltpu.VMEM((tq, tk), F32)]
    return pl.pallas_call(
        functools.partial(_attn_kernel, layer=layer, n_lat=n_lat, tk=tk),
        out_shape=jax.ShapeDtypeStruct((b, s, ATT_W), BF16),
        grid=(b, N_HEADS, s // tq),
        in_specs=in_specs,
        out_specs=pl.BlockSpec((1, tq, V_DIM), lambda bi, h, i: (bi, i, h)),
        scratch_shapes=scratch,
        compiler_params=_params(("parallel", "parallel", "arbitrary")),
        name="attention",
    )(*args)


def _sgu_kernel(p_ref, gn_ref, w_ref, b_ref, o_ref, *, n_chunks):
    for c in range(n_chunks):
        rows = slice(c * SGU_CH, (c + 1) * SGU_CH)
        z = _gelu_tanh(p_ref[0, rows, :].astype(F32))
        u = z[:, :SGU_W]
        vn = _rms(z[:, SGU_W:], gn_ref[...]).astype(BF16)
        for g in range(SGU_G):
            cols = slice(g * SGU_CH, (g + 1) * SGU_CH)
            mixed = _dot(w_ref[g], vn[:, cols]) + b_ref[g]
            o_ref[0, rows, cols] = (u[:, cols] * mixed).astype(o_ref.dtype)


def _sgu(p, gn, w_s, b_b, layer, rows):
    b, s, _ = p.shape
    return pl.pallas_call(
        functools.partial(_sgu_kernel, n_chunks=rows // SGU_CH),
        out_shape=jax.ShapeDtypeStruct((b, s, SGU_W), BF16),
        grid=(b, s // rows),
        in_specs=[pl.BlockSpec((1, rows, 2 * SGU_W), lambda bi, i: (bi, i, 0)),
                  _layer_block(layer, 1, SGU_W),
                  _layer_block(layer, SGU_G, SGU_CH, SGU_CH),
                  _layer_block(layer, SGU_G, SGU_CH, SGU_CH)],
        out_specs=pl.BlockSpec((1, rows, SGU_W), lambda bi, i: (bi, i, 0)),
        compiler_params=_params(("parallel", "parallel")),
        name="sgu",
    )(p, gn, w_s, b_b)


def _prep_kernel(cq_ref, anq_ref, wq_ref, gq_ref, gq2_ref, t_ref, ankv_ref, wkv_ref, gk_ref, gk2_ref, cs_ref,
                 p_ref, gn_ref, ws_ref, bs_ref, q_ref, k_ref, v_ref, sgu_ref, *, n_chunks):
    _q_proj_kernel(cq_ref, anq_ref, wq_ref, gq_ref, gq2_ref, cs_ref, q_ref)
    _kv_proj_kernel(t_ref, ankv_ref, wkv_ref, gk_ref, gk2_ref, cs_ref, k_ref, v_ref)
    _sgu_kernel(p_ref, gn_ref, ws_ref, bs_ref, sgu_ref, n_chunks=n_chunks)


def _prep(cq, an_q, w_uq_p, gq, gq2, t, an_kv, w_ukv, gk, gk2, cs, p, gn, w_s, b_b, layer, tm):
    b, s, _ = cq.shape
    width = N_HEADS * HEAD_PAD
    row = lambda w: pl.BlockSpec((1, tm, w), lambda bi, i: (bi, i, 0))
    return pl.pallas_call(
        functools.partial(_prep_kernel, n_chunks=tm // SGU_CH),
        out_shape=[jax.ShapeDtypeStruct((b, s, width), BF16), jax.ShapeDtypeStruct((b, s, width), BF16),
                   jax.ShapeDtypeStruct((b, s, width), BF16), jax.ShapeDtypeStruct((b, s, SGU_W), BF16)],
        grid=(b, s // tm),
        in_specs=[row(Q_LORA), _layer_block(layer, 1, Q_LORA), _layer_block(layer, Q_LORA, N_HEADS * Q_HEAD_COLS),
                  _layer_block(layer, 1, NOPE), _layer_block(layer, 1, 2 * V7X_LANES),
                  row(t.shape[2]), _layer_block(layer, 1, KV_LORA), _layer_block(layer, KV_LORA, N_HEADS * 2 * NOPE),
                  _layer_block(layer, 1, NOPE), _layer_block(layer, 1, 2 * V7X_LANES),
                  pl.BlockSpec((tm, 2 * V7X_LANES), lambda bi, i: (i, 0)),
                  row(2 * SGU_W), _layer_block(layer, 1, SGU_W), _layer_block(layer, SGU_G, SGU_CH, SGU_CH),
                  _layer_block(layer, SGU_G, SGU_CH, SGU_CH)],
        out_specs=[row(width), row(width), row(width), row(SGU_W)],
        compiler_params=_params(("parallel", "parallel")),
        name="prep",
    )(cq, an_q, w_uq_p, gq, gq2, t, an_kv, w_ukv, gk, gk2, cs, p, gn, w_s, b_b)


def _ssd_direction(x_ref, dt_ref, bias_ref, alog_ref, tri_ref, exp_ref, state_ref, y_ref, *, bi, reverse):
    q = SSD_Q
    xs = x_ref[bi, :, :SSD_IN].astype(F32)
    dt = _softplus(dt_ref[bi] + bias_ref[...])
    a_dt = dt * (-jnp.exp(alog_ref[...]))
    tri = tri_ref[...]
    acs = sum(_dot(tri, part) for part in _split3(a_dt))
    acs_t = acs.T
    expand = exp_ref[...]
    acs_x = sum(_dot(part, expand) for part in _split3(acs))
    dt_hi = dt.astype(BF16)
    dt_x = _dot(dt_hi, expand) + _dot((dt - dt_hi.astype(F32)).astype(BF16), expand)
    last = acs_x[0:1, :] if reverse else acs_x[q - 1:q, :]
    xdt = xs * dt_x
    x_end = (xdt * jnp.exp(last - acs_x)).astype(BF16)
    decay_out = jnp.exp(acs_x)
    chunk_decay = jnp.exp(last)

    ti = lax.broadcasted_iota(jnp.int32, (q, q), 0)
    tj = lax.broadcasted_iota(jnp.int32, (q, q), 1)
    keep = (ti <= tj) if reverse else (ti >= tj)
    lane_blk = lax.shift_right_logical(lax.broadcasted_iota(jnp.int32, (q, GRP_W), 1), int(math.log2(SSD_P)))
    head0 = SSD_H if reverse else 0
    heads_per_group = SSD_H // SSD_G
    for g in range(SSD_G):
        bm = x_ref[bi, :, SSD_IN + g * SSD_N:SSD_IN + (g + 1) * SSD_N]
        cm = x_ref[bi, :, SSD_IN + (SSD_G + g) * SSD_N:SSD_IN + (SSD_G + g + 1) * SSD_N]
        cb = _dot_nt(cm, bm)
        cols = slice(g * GRP_W, (g + 1) * GRP_W)
        state = state_ref[g]
        y_off = _dot(cm, state.astype(BF16)) * decay_out[:, cols]
        state_ref[g] = state * chunk_decay[:, cols] + _dot(bm.astype(F32).T.astype(BF16), x_end[:, cols])
        blocks, stacked = [], []
        for hh in range(heads_per_group):
            hcol = head0 + g * heads_per_group + hh
            seg = acs[:, hcol:hcol + 1] - acs_t[hcol:hcol + 1, :]
            blocks.append((cb * jnp.exp(jnp.where(keep, seg, -jnp.inf))).astype(BF16))
            stacked.append(jnp.where(lane_blk == hh, xdt[:, cols], 0.0).astype(BF16))
        y = y_off + _dot(jnp.concatenate(blocks, axis=1), jnp.concatenate(stacked, axis=0))
        y_ref[bi, :, cols] = y.astype(y_ref.dtype)


def _ssd_kernel(xf_ref, dtf_ref, xb_ref, dtb_ref, bias_ref, alog_ref, tri_ref, exp_ref, init_ref,
                yf_ref, yb_ref, fin_ref, state_ref):
    s = pl.program_id(0)

    @pl.when(s == 0)
    def _():
        state_ref[...] = init_ref[...]

    for bi in range(xf_ref.shape[0]):
        _ssd_direction(xf_ref, dtf_ref, bias_ref, alog_ref, tri_ref.at[0], exp_ref.at[0], state_ref.at[bi, 0],
                       yf_ref, bi=bi, reverse=False)
        _ssd_direction(xb_ref, dtb_ref, bias_ref, alog_ref, tri_ref.at[1], exp_ref.at[1], state_ref.at[bi, 1],
                       yb_ref, bi=bi, reverse=True)

    @pl.when(s == pl.num_programs(0) - 1)
    def _():
        fin_ref[...] = state_ref[...]


def _ssd(xbc, dt, bias, alog, tri, expand, init, layer):
    b, s, _ = xbc.shape
    nc = s // SSD_Q

    def main_f(i): return (0, i, 0)
    def main_b(i): return (0, nc - 1 - i, 0)
    const3 = lambda i: (0, 0, 0)
    state_shape = (b, 2, SSD_G, SSD_N, GRP_W)
    state_block = pl.BlockSpec(state_shape, lambda i: (0, 0, 0, 0, 0))
    return pl.pallas_call(
        _ssd_kernel,
        out_shape=[jax.ShapeDtypeStruct((b, s, SSD_IN), BF16), jax.ShapeDtypeStruct((b, s, SSD_IN), BF16),
                   jax.ShapeDtypeStruct(state_shape, F32)],
        grid=(nc,),
        in_specs=[pl.BlockSpec((b, SSD_Q, XBC_W), main_f), pl.BlockSpec((b, SSD_Q, V7X_LANES), main_f),
                  pl.BlockSpec((b, SSD_Q, XBC_W), main_b), pl.BlockSpec((b, SSD_Q, V7X_LANES), main_b),
                  _layer_block(layer, 1, V7X_LANES), _layer_block(layer, 1, V7X_LANES),
                  pl.BlockSpec((2, SSD_Q, SSD_Q), const3), pl.BlockSpec((2, V7X_LANES, SSD_IN), const3),
                  state_block],
        out_specs=[pl.BlockSpec((b, SSD_Q, SSD_IN), main_f), pl.BlockSpec((b, SSD_Q, SSD_IN), main_b),
                   state_block],
        scratch_shapes=[pltpu.VMEM(state_shape, F32)],
        compiler_params=_params(("arbitrary",)),
        name="ssd",
    )(xbc, dt, xbc, dt, bias, alog, tri, expand, init)


def _out_proj_kernel(x_ref, mod_ref, att_ref, sgu_ref, yf_ref, yb_ref, xs_ref, z_ref,
                     ga_ref, gs_ref, d_ref, gy_ref, gf_ref, w_ref, o_ref, h_ref):
    att = _rms(att_ref[0].astype(F32), ga_ref[...]).astype(BF16)
    sgu = _rms(sgu_ref[0].astype(F32), gs_ref[...]).astype(BF16)
    y = yf_ref[0].astype(F32) + yb_ref[0].astype(F32) + d_ref[...] * xs_ref[0].astype(F32)
    ssd = _rms(y * _silu(z_ref[0].astype(F32)), gy_ref[...]).astype(BF16)
    mix = (_dot(att, w_ref[:ATT_W, :]) + _dot(sgu, w_ref[ATT_W:ATT_W + SGU_W, :])
           + _dot(ssd, w_ref[ATT_W + SGU_W:, :]))
    x_new = x_ref[0] + mod_ref[0, 2:3, :] * mix
    o_ref[0] = x_new
    h_ref[0] = (_rms(x_new, gf_ref[...]) * (1.0 + mod_ref[0, 4:5, :]) + mod_ref[0, 3:4, :]).astype(BF16)


def _out_proj(x, mods, mod_row, att, sgu, yf, yb, xs, z, ga, gs, dvec, gy, gf, w_out, layer, tm):
    b, s, d = x.shape
    row = lambda w: pl.BlockSpec((1, tm, w), lambda bi, i: (bi, i, 0))
    vec = lambda w: _layer_block(layer, 1, w)
    return pl.pallas_call(
        _out_proj_kernel,
        out_shape=[jax.ShapeDtypeStruct((b, s, d), F32), jax.ShapeDtypeStruct((b, s, d), BF16)],
        grid=(b, s // tm),
        in_specs=[row(d), _mod_block(layer, d, mod_row),
                  row(ATT_W), row(SGU_W), row(SSD_IN), row(SSD_IN), row(SSD_IN), row(SSD_IN),
                  vec(ATT_W), vec(SGU_W), vec(SSD_IN), vec(SSD_IN), vec(d),
                  pl.BlockSpec((None,) + w_out.shape[1:], lambda bi, i: (layer, 0, 0),
                               pipeline_mode=pl.Buffered(1))],
        out_specs=[row(d), row(d)],
        compiler_params=_params(("parallel", "parallel")),
        name="out_proj",
    )(x, mods, att, sgu, yf, yb, xs, z, ga, gs, dvec, gy, gf, w_out)


def _ffn_kernel(x_ref, h_ref, hp_ref, hn_ref, mod_ref, wg_ref, wu_ref, cw_ref, cb_ref, wd_ref, o_ref,
                halo_sc, acc_sc, *, seq_rows):
    i = pl.program_id(1)
    j = pl.program_id(2)
    hb = BF16_ROWS

    @pl.when(j == 0)
    def _():
        halo_sc[0:hb, :] = jnp.where(i > 0, hp_ref[0], jnp.zeros_like(hp_ref[0]))
        halo_sc[hb:, :] = jnp.where(i < pl.num_programs(1) - 1, hn_ref[0], jnp.zeros_like(hn_ref[0]))
        acc_sc[...] = jnp.zeros_like(acc_sc)

    h = h_ref[0]
    tm = h.shape[0]
    gate = _dot(h, wg_ref[...])
    gate_halo = _dot(halo_sc[...], wg_ref[...])
    row = lax.broadcasted_iota(jnp.int32, gate.shape, 0)
    up = jnp.where(row == 0, gate_halo[hb - 1:hb, :], pltpu.roll(gate, 1, axis=0))
    dn = jnp.where(row == tm - 1, gate_halo[hb:hb + 1, :], pltpu.roll(gate, tm - 1, axis=0))
    if seq_rows is not None:
        pos = lax.rem(row, seq_rows)
        up = jnp.where(pos == 0, 0.0, up)
        dn = jnp.where(pos == seq_rows - 1, 0.0, dn)
    conv = cb_ref[...] + cw_ref[0:1, :] * up + cw_ref[1:2, :] * gate + cw_ref[2:3, :] * dn
    act = (_silu(conv) * _dot(h, wu_ref[...])).astype(BF16)
    acc_sc[...] += _dot(act, wd_ref[...])

    @pl.when(j == pl.num_programs(2) - 1)
    def _():
        o_ref[0] = x_ref[0] + mod_ref[0, 5:6, :] * acc_sc[...]


def _ffn(x, h, mods, mod_row, w_gate, w_up, conv_w, conv_b, w_down, layer, tm, tf, pack_sequences=False):
    out_shape = x.shape
    seq_rows = None
    if pack_sequences:
        assert mod_row is not None
        seq_rows, d = x.shape[1], x.shape[2]
        x, h = x.reshape(1, -1, d), h.reshape(1, -1, d)
        tm = x.shape[1]
    b, s, d = x.shape
    f = w_down.shape[1]
    per = tm // BF16_ROWS
    last_blk = s // BF16_ROWS - 1
    assert f % tf == 0
    return pl.pallas_call(
        functools.partial(_ffn_kernel, seq_rows=seq_rows),
        out_shape=jax.ShapeDtypeStruct((b, s, d), F32),
        grid=(b, s // tm, f // tf),
        in_specs=[pl.BlockSpec((1, tm, d), lambda bi, i, j: (bi, i, 0)),
                  pl.BlockSpec((1, tm, d), lambda bi, i, j: (bi, i, 0)),
                  pl.BlockSpec((1, BF16_ROWS, d), lambda bi, i, j: (bi, jnp.maximum(i * per - 1, 0), 0)),
                  pl.BlockSpec((1, BF16_ROWS, d), lambda bi, i, j: (bi, jnp.minimum((i + 1) * per, last_blk), 0)),
                  _mod_block(layer, d, mod_row),
                  pl.BlockSpec((None, d, tf), lambda bi, i, j: (layer, 0, j)),
                  pl.BlockSpec((None, d, tf), lambda bi, i, j: (layer, 0, j)),
                  pl.BlockSpec((None, 3, tf), lambda bi, i, j: (layer, 0, j)),
                  pl.BlockSpec((None, 1, tf), lambda bi, i, j: (layer, 0, j)),
                  pl.BlockSpec((None, tf, d), lambda bi, i, j: (layer, j, 0))],
        out_specs=pl.BlockSpec((1, tm, d), lambda bi, i, j: (bi, i, 0)),
        scratch_shapes=[pltpu.VMEM((2 * BF16_ROWS, d), BF16), pltpu.VMEM((tm, d), F32)],
        compiler_params=_params(("parallel", "parallel", "arbitrary")),
        name="ffn",
    )(x, h, h, h, mods, w_gate, w_up, conv_w, conv_b, w_down).reshape(out_shape)


def _rope_tables(seq):
    rows = seq // GRID_COLS
    pairs = ROPE // 4
    freqs = ROPE_THETA ** (-jnp.arange(pairs, dtype=F32) / pairs)
    ar = jnp.arange(rows, dtype=F32)[:, None] * freqs
    ac = jnp.arange(GRID_COLS, dtype=F32)[:, None] * freqs
    by_row = lambda t: jnp.repeat(t, GRID_COLS, axis=0)
    by_col = lambda t: jnp.tile(t, (rows, 1))
    cr, sr, cc, sc = by_row(jnp.cos(ar)), by_row(jnp.sin(ar)), by_col(jnp.cos(ac)), by_col(jnp.sin(ac))
    cos = jnp.concatenate([cr, cr, cc, cc], axis=-1)
    sin = jnp.concatenate([-sr, sr, -sc, sc], axis=-1)
    zeros = jnp.zeros_like(cos)
    return jnp.concatenate([cos, zeros, sin, zeros], axis=-1)


def _pack_w_in(w):
    w = w.astype(BF16)
    off_kr, off_sgu = Q_LORA + KV_LORA, Q_LORA + KV_LORA + ROPE
    off_dt = off_sgu + 2 * SGU_W + SSD_IN + XBC_W
    kr = w[..., off_kr:off_sgu]
    dt = w[..., off_dt:]
    z64 = jnp.zeros(w.shape[:2] + (V7X_LANES - ROPE,), BF16)
    return jnp.concatenate([w[..., :off_kr], kr, z64, kr[..., _ROT_PERM], z64, w[..., off_sgu:off_dt], dt,
                            jnp.zeros(w.shape[:2] + (V7X_LANES - dt.shape[-1],), BF16)], axis=-1)


def _pack_w_uq(w):
    depth = w.shape[0]
    pairs = N_HEADS // 2
    w = w.astype(BF16).reshape(depth, Q_LORA, pairs, 2, QK)
    rot = w[..., NOPE:]
    z64 = jnp.zeros((depth, Q_LORA, pairs, 2, V7X_LANES - ROPE), BF16)
    both = lambda t: t.reshape(depth, Q_LORA, pairs, 2 * V7X_LANES)
    return jnp.concatenate([both(w[..., :NOPE]), both(jnp.concatenate([rot, z64], axis=-1)),
                            both(jnp.concatenate([rot[..., _ROT_PERM], z64], axis=-1))],
                           axis=-1).reshape(depth, Q_LORA, N_HEADS * Q_HEAD_COLS)


def _rot_gains(g):
    rot = g[:, NOPE:]
    z64 = jnp.zeros((g.shape[0], V7X_LANES - ROPE), g.dtype)
    return jnp.concatenate([rot, z64, rot[:, _ROT_PERM], z64], axis=-1)[:, None, :]


def _row_vectors(v, width=None):
    v = v.reshape(v.shape[0], 1, -1)
    return v if width is None else jnp.pad(v, ((0, 0), (0, 0), (0, width - v.shape[-1])))


def kernel(x, c, ctx, c_ctx, w_mod, b_mod, norm_mix, norm_ffn, w_in, q_a_norm, w_uq, kv_a_norm, w_ukv, q_norm, k_norm, attn_out_norm, sgu_norm, sgu_w, sgu_b, gmlp_out_norm, ssd_conv_w, ssd_conv_b, ssd_dt_bias, ssd_a_log, ssd_d, ssd_norm, w_out, ffn_w_gate, ffn_w_up, ffn_conv_w, ffn_conv_b, ffn_w_down):
    batch, seq, d = x.shape
    n_ctx = ctx.shape[1]
    depth = w_mod.shape[0]
    tm = min(512, seq)
    tm_ctx = min(256, n_ctx)
    tf = 512
    tk_att = 1024
    tq_att = 1024

    cc = jnp.concatenate([c, c_ctx[None], jnp.zeros((V7X_SUBLANES - batch - 1, d), F32)], axis=0)
    mods = jnp.pad(_mods(cc, w_mod, b_mod).reshape(depth, V7X_SUBLANES, 6, d),
                   ((0, 0), (0, 0), (0, V7X_SUBLANES - 6), (0, 0)))
    ctx_row = batch

    cs_lat = _rope_tables(seq)
    cs_ctx = jnp.concatenate([jnp.ones((n_ctx, ROPE), F32), jnp.zeros((n_ctx, 2 * V7X_LANES - ROPE), F32)], axis=-1)

    ti = np.arange(SSD_Q)
    tri = jnp.asarray(np.stack([ti[:, None] >= ti[None, :], ti[:, None] <= ti[None, :]]), BF16)
    lane_head = np.arange(SSD_IN) // SSD_P
    col = np.arange(V7X_LANES)
    expand = jnp.asarray(np.stack([col[:, None] == lane_head[None, :],
                                   col[:, None] == lane_head[None, :] + SSD_H]), BF16)

    w_in_p, w_uq_p = _pack_w_in(w_in), _pack_w_uq(w_uq)
    w_ukv_b, w_out_b = w_ukv.astype(BF16), w_out.astype(BF16)
    wg, wu, wd = ffn_w_gate.astype(BF16), ffn_w_up.astype(BF16), ffn_w_down.astype(BF16)

    g_mix, g_ffn = _row_vectors(norm_mix), _row_vectors(norm_ffn)
    an_q, an_kv = _row_vectors(q_a_norm), _row_vectors(kv_a_norm)
    gq1, gk1 = _row_vectors(q_norm[:, :NOPE]), _row_vectors(k_norm[:, :NOPE])
    gq2, gk2 = _rot_gains(q_norm), _rot_gains(k_norm)
    g_att, g_sgu_in, g_sgu_out = _row_vectors(attn_out_norm), _row_vectors(sgu_norm), _row_vectors(gmlp_out_norm)
    g_ssd = _row_vectors(ssd_norm)
    sgu_w_b = sgu_w.astype(BF16)
    sgu_b_b = jnp.broadcast_to(sgu_b[..., None], sgu_b.shape + (SGU_CH,))
    conv_b = _row_vectors(ssd_conv_b)
    bias, alog = _row_vectors(ssd_dt_bias, V7X_LANES), _row_vectors(ssd_a_log, V7X_LANES)
    dvec = _row_vectors(jnp.repeat(ssd_d, SSD_P, axis=1))
    fcb = _row_vectors(ffn_conv_b)
    score_bound = (QK * SM_SCALE * LOG2_E) * jnp.max(jnp.abs(q_norm), axis=1) * jnp.max(jnp.abs(k_norm), axis=1)
    bounded = (score_bound <= SCORE_BOUND_LIMIT).astype(jnp.int32)

    x_lat, x_ctx = x, ctx
    for l in range(depth):
        need_ctx = l < depth - 1

        def mixers(xs_in, mod_row, cs, t_rows, init_state, k_c=None, v_c=None, outputs=True):
            pq, pkv, psgu, pz, xbc, pdt = _in_proj(xs_in, mods, mod_row, g_mix, w_in_p, ssd_conv_w, conv_b, l, t_rows)
            yf, yb, fin = _ssd(xbc, pdt, bias, alog, tri, expand, init_state, l)
            if not outputs:
                k, v = _kv_proj(pkv, an_kv, w_ukv_b, l, gk1, gk2, cs, t_rows)
                return None, k, v, fin
            q, k, v, sgu = _prep(pq, an_q, w_uq_p, gq1, gq2, pkv, an_kv, w_ukv_b, gk1, gk2, cs,
                                 psgu, g_sgu_in, sgu_w_b, sgu_b_b, l, t_rows)
            if k_c is None:
                att = _attention(bounded, l, q, k, v, tq=t_rows, tk=tk_att)
            else:
                att = _attention(bounded, l, q, k_c, v_c, k, v, tq=min(tq_att, q.shape[1]), tk=tk_att)
            out = _out_proj(xs_in, mods, mod_row, att, sgu, yf, yb, xbc, pz, g_att, g_sgu_out, dvec, g_ssd, g_ffn,
                            w_out_b, l, t_rows)
            return out, k, v, fin

        zero_state = jnp.zeros((batch, 2, SSD_G, SSD_N, GRP_W), F32)
        ctx_mid, k_c, v_c, ctx_state = mixers(x_ctx, ctx_row, cs_ctx, tm_ctx, zero_state, outputs=need_ctx)
        (x_lat, h_lat), _, _, _ = mixers(x_lat, None, cs_lat, tm, ctx_state, k_c, v_c)
        x_lat = _ffn(x_lat, h_lat, mods, None, wg, wu, ffn_conv_w, fcb, wd, l, tm, tf)
        if need_ctx:
            x_ctx = _ffn(ctx_mid[0], ctx_mid[1], mods, ctx_row, wg, wu, ffn_conv_w, fcb, wd, l, tm_ctx, tf,
                         pack_sequences=True)
    return x_lat
```

```python
import functools
import math

import jax
import jax.numpy as jnp
import numpy as np
from jax import lax
from jax.experimental import pallas as pl
from jax.experimental.pallas import tpu as pltpu

F32 = jnp.float32
BF16 = jnp.bfloat16
NORM_EPS = 1e-6

GRID_COLS = 64
N_HEADS = 8
NOPE = 128
ROPE = 64
QK = NOPE + ROPE
V_DIM = 128
Q_LORA = 768
KV_LORA = 512
ROPE_THETA = 10000.0
SM_SCALE = 1.0 / math.sqrt(QK)
LOG2_E = math.log2(math.e)
SCORE_BOUND_LIMIT = 60.0
HEAD_PAD = 256
ATT_W = N_HEADS * V_DIM
SGU_G = 4
SGU_CH = 128
SGU_W = 512
SSD_H = 8
SSD_P = 64
SSD_IN = SSD_H * SSD_P
SSD_N = 128
SSD_G = 2
SSD_Q = 128
XBC_W = SSD_IN + 2 * SSD_G * SSD_N
GRP_W = (SSD_H // SSD_G) * SSD_P

V7X_LANES = 128
V7X_SUBLANES = 8
BF16_ROWS = 2 * V7X_SUBLANES
V7X_VMEM_BYTES = 64 * 1024 * 1024
VMEM_LIMIT = V7X_VMEM_BYTES - 8 * 1024 * 1024

SEG_Q = (0, Q_LORA)
SEG_KV = (SEG_Q[1], SEG_Q[1] + KV_LORA + 2 * V7X_LANES)
SEG_SGU = (SEG_KV[1], SEG_KV[1] + 2 * SGU_W)
SEG_Z = (SEG_SGU[1], SEG_SGU[1] + SSD_IN)
SEG_XBC = (SEG_Z[1], SEG_Z[1] + XBC_W)
SEG_DT = (SEG_XBC[1], SEG_XBC[1] + V7X_LANES)
P_PAD = SEG_DT[1]
Q_HEAD_COLS = NOPE + 2 * V7X_LANES

_ROT_PERM = np.concatenate([np.arange(16, 32), np.arange(0, 16), np.arange(48, 64), np.arange(32, 48)])


def _params(sem, vmem=VMEM_LIMIT):
    return pltpu.CompilerParams(dimension_semantics=sem, vmem_limit_bytes=vmem)


def _layer_block(layer, *shape):
    zeros = (0,) * len(shape)
    return pl.BlockSpec((None,) + shape, lambda *_: (layer,) + zeros)


def _mod_block(layer, d, fixed_row):
    if fixed_row is None:
        return pl.BlockSpec((None, 1, V7X_SUBLANES, d), lambda bi, *_: (layer, bi, 0, 0))
    return pl.BlockSpec((None, 1, V7X_SUBLANES, d), lambda *_: (layer, fixed_row, 0, 0))


def _dot(a, b):
    return jnp.dot(a, b, preferred_element_type=F32)


def _dot_nt(a, b):
    return lax.dot_general(a, b, (((1,), (1,)), ((), ())), preferred_element_type=F32)


def _rms(t, gain):
    return t * lax.rsqrt(jnp.mean(t * t, axis=-1, keepdims=True) + NORM_EPS) * gain


def _silu(t):
    return t * (1.0 / (1.0 + jnp.exp(-t)))


def _gelu_tanh(t):
    return 0.5 * t * (1.0 + jnp.tanh(math.sqrt(2.0 / math.pi) * (t + 0.044715 * (t * t * t))))


def _softplus(t):
    return jnp.maximum(t, 0.0) + jnp.log1p(jnp.exp(-jnp.abs(t)))


def _split3(t):
    hi = t.astype(BF16)
    r1 = t - hi.astype(F32)
    mid = r1.astype(BF16)
    lo = (r1 - mid.astype(F32)).astype(BF16)
    return hi, mid, lo


def _mods_kernel(c_ref, w_ref, b_ref, o_ref):
    s = _silu(c_ref[...])
    s_hi = s.astype(BF16)
    s_lo = (s - s_hi.astype(F32)).astype(BF16)
    w = w_ref[0]
    w_hi = w.astype(BF16)
    w_lo = (w - w_hi.astype(F32)).astype(BF16)
    o_ref[0] = _dot(s_hi, w_hi) + _dot(s_lo, w_hi) + _dot(s_hi, w_lo) + b_ref[0]


def _mods(cc, w_mod, b_mod):
    depth, d, n6 = w_mod.shape
    tn = 1024
    return pl.pallas_call(
        _mods_kernel,
        out_shape=jax.ShapeDtypeStruct((depth, V7X_SUBLANES, n6), F32),
        grid=(depth, n6 // tn),
        in_specs=[pl.BlockSpec((V7X_SUBLANES, d), lambda l, j: (0, 0)),
                  pl.BlockSpec((1, d, tn), lambda l, j: (l, 0, j)),
                  pl.BlockSpec((1, 1, tn), lambda l, j: (l, 0, j))],
        out_specs=pl.BlockSpec((1, V7X_SUBLANES, tn), lambda l, j: (l, 0, j)),
        compiler_params=_params(("parallel", "parallel")),
        name="mods",
    )(cc, w_mod, b_mod.reshape(depth, 1, n6))


def _in_proj_kernel(x_ref, xp_ref, xn_ref, mod_ref, g_ref, w_ref, cw_ref, cb_ref,
                    q_ref, kv_ref, sgu_ref, z_ref, xbc_ref, dt_ref):
    i = pl.program_id(1)
    shift = mod_ref[0, 0:1, :]
    scale = mod_ref[0, 1:2, :]

    def norm_mod(t):
        return _rms(t, g_ref[...]) * (1.0 + scale) + shift

    a, b = SEG_XBC
    halo = jnp.concatenate([norm_mod(xp_ref[0]), norm_mod(xn_ref[0])], axis=0).astype(BF16)
    pre_halo = _dot(halo, w_ref[:, a:b])
    prev_row = jnp.where(i > 0, pre_halo[V7X_SUBLANES - 1:V7X_SUBLANES, :], 0.0)
    next_row = jnp.where(i < pl.num_programs(1) - 1, pre_halo[V7X_SUBLANES:V7X_SUBLANES + 1, :], 0.0)

    h = norm_mod(x_ref[0]).astype(BF16)
    tm = h.shape[0]
    pre = _dot(h, w_ref[:, a:b])
    row = lax.broadcasted_iota(jnp.int32, pre.shape, 0)
    up = jnp.where(row == 0, prev_row, pltpu.roll(pre, 1, axis=0))
    dn = jnp.where(row == tm - 1, next_row, pltpu.roll(pre, tm - 1, axis=0))
    conv = cb_ref[...] + cw_ref[0:1, :] * up + cw_ref[1:2, :] * pre + cw_ref[2:3, :] * dn
    xbc_ref[0] = _silu(conv).astype(xbc_ref.dtype)

    for ref, (a, b) in ((q_ref, SEG_Q), (kv_ref, SEG_KV), (sgu_ref, SEG_SGU), (z_ref, SEG_Z), (dt_ref, SEG_DT)):
        ref[0] = _dot(h, w_ref[:, a:b]).astype(ref.dtype)


def _in_proj(x, mods, mod_row, gain, w_in_p, conv_w, conv_b, layer, tm):
    b, s, d = x.shape
    segs = (SEG_Q, SEG_KV, SEG_SGU, SEG_Z, SEG_XBC, SEG_DT)
    dts = (BF16, BF16, BF16, BF16, BF16, F32)
    per = tm // V7X_SUBLANES
    last_blk = s // V7X_SUBLANES - 1
    return pl.pallas_call(
        _in_proj_kernel,
        out_shape=[jax.ShapeDtypeStruct((b, s, hi - lo), dt) for (lo, hi), dt in zip(segs, dts)],
        grid=(b, s // tm),
        in_specs=[pl.BlockSpec((1, tm, d), lambda bi, i: (bi, i, 0)),
                  pl.BlockSpec((1, V7X_SUBLANES, d), lambda bi, i: (bi, jnp.maximum(i * per - 1, 0), 0)),
                  pl.BlockSpec((1, V7X_SUBLANES, d), lambda bi, i: (bi, jnp.minimum((i + 1) * per, last_blk), 0)),
                  _mod_block(layer, d, mod_row),
                  _layer_block(layer, 1, d),
                  pl.BlockSpec((None, d, P_PAD), lambda bi, i: (layer, 0, 0), pipeline_mode=pl.Buffered(1)),
                  _layer_block(layer, 3, XBC_W),
                  _layer_block(layer, 1, XBC_W)],
        out_specs=[pl.BlockSpec((1, tm, hi - lo), lambda bi, i: (bi, i, 0)) for lo, hi in segs],
        compiler_params=_params(("parallel", "parallel")),
        name="in_proj",
    )(x, x, x, mods, gain, w_in_p, conv_w, conv_b)


def _q_proj_kernel(cq_ref, an_ref, w_ref, g_ref, g2_ref, cs_ref, q_ref):
    cq = cq_ref[0].astype(F32)
    cqn = _rms(cq, an_ref[...]).astype(BF16)
    gcs = g2_ref[...] * cs_ref[...]
    gc, gs = gcs[:, :V7X_LANES], gcs[:, V7X_LANES:]
    lanes = V7X_LANES
    for pair in range(N_HEADS // 2):
        y = _dot(cqn, w_ref[:, pair * 2 * Q_HEAD_COLS:(pair + 1) * 2 * Q_HEAD_COLS])
        for e in range(2):
            h = 2 * pair + e
            a = y[:, e * lanes:(e + 1) * lanes]
            r = y[:, (2 + e) * lanes:(3 + e) * lanes]
            rp = y[:, (4 + e) * lanes:(5 + e) * lanes]
            ssq = jnp.sum(a * a + r * r, axis=-1, keepdims=True)
            rs = lax.rsqrt(ssq * (1.0 / QK) + NORM_EPS) * (SM_SCALE * LOG2_E)
            q_ref[0, :, h * HEAD_PAD:h * HEAD_PAD + NOPE] = (a * g_ref[...] * rs).astype(BF16)
            q_ref[0, :, h * HEAD_PAD + NOPE:(h + 1) * HEAD_PAD] = ((r * gc + rp * gs) * rs).astype(BF16)


def _q_proj(cq, an, w_uq_p, layer, g, g2, cs, tm):
    b, s, _ = cq.shape
    width = N_HEADS * HEAD_PAD
    return pl.pallas_call(
        _q_proj_kernel,
        out_shape=jax.ShapeDtypeStruct((b, s, width), BF16),
        grid=(b, s // tm),
        in_specs=[pl.BlockSpec((1, tm, Q_LORA), lambda bi, i: (bi, i, 0)),
                  _layer_block(layer, 1, Q_LORA),
                  _layer_block(layer, Q_LORA, N_HEADS * Q_HEAD_COLS),
                  _layer_block(layer, 1, NOPE),
                  _layer_block(layer, 1, 2 * V7X_LANES),
                  pl.BlockSpec((tm, 2 * V7X_LANES), lambda bi, i: (i, 0))],
        out_specs=pl.BlockSpec((1, tm, width), lambda bi, i: (bi, i, 0)),
        compiler_params=_params(("parallel", "parallel")),
        name="q_proj",
    )(cq, an, w_uq_p, g, g2, cs)


def _kv_proj_kernel(t_ref, an_ref, w_ref, g_ref, g2_ref, cs_ref, k_ref, v_ref):
    ckv = t_ref[0, :, :KV_LORA].astype(F32)
    kr = t_ref[0, :, KV_LORA:KV_LORA + V7X_LANES].astype(F32)
    krp = t_ref[0, :, KV_LORA + V7X_LANES:].astype(F32)
    ckvn = _rms(ckv, an_ref[...]).astype(BF16)
    gcs = g2_ref[...] * cs_ref[...]
    rot = kr * gcs[:, :V7X_LANES] + krp * gcs[:, V7X_LANES:]
    kr_sq = kr * kr
    lane = lax.broadcasted_iota(jnp.int32, kr.shape, 1)
    ones_col = jnp.where(lane == 0, 1.0, 0.0).astype(BF16)
    for h in range(N_HEADS):
        y = _dot(ckvn, w_ref[:, h * 2 * NOPE:(h + 1) * 2 * NOPE])
        kn = y[:, :NOPE]
        rs = lax.rsqrt(jnp.sum(kn * kn + kr_sq, axis=-1, keepdims=True) * (1.0 / QK) + NORM_EPS)
        k_ref[0, :, h * HEAD_PAD:h * HEAD_PAD + NOPE] = (kn * g_ref[...] * rs).astype(BF16)
        k_ref[0, :, h * HEAD_PAD + NOPE:(h + 1) * HEAD_PAD] = (rot * rs).astype(BF16)
        v_ref[0, :, h * HEAD_PAD:h * HEAD_PAD + V_DIM] = y[:, NOPE:].astype(BF16)
        v_ref[0, :, h * HEAD_PAD + V_DIM:(h + 1) * HEAD_PAD] = ones_col


def _kv_proj(t, an, w_ukv, layer, g, g2, cs, tm):
    b, s, wt = t.shape
    return pl.pallas_call(
        _kv_proj_kernel,
        out_shape=[jax.ShapeDtypeStruct((b, s, N_HEADS * HEAD_PAD), BF16),
                   jax.ShapeDtypeStruct((b, s, N_HEADS * HEAD_PAD), BF16)],
        grid=(b, s // tm),
        in_specs=[pl.BlockSpec((1, tm, wt), lambda bi, i: (bi, i, 0)),
                  _layer_block(layer, 1, KV_LORA),
                  _layer_block(layer, KV_LORA, N_HEADS * 2 * NOPE),
                  _layer_block(layer, 1, NOPE),
                  _layer_block(layer, 1, 2 * V7X_LANES),
                  pl.BlockSpec((tm, 2 * V7X_LANES), lambda bi, i: (i, 0))],
        out_specs=[pl.BlockSpec((1, tm, N_HEADS * HEAD_PAD), lambda bi, i: (bi, i, 0)),
                   pl.BlockSpec((1, tm, N_HEADS * HEAD_PAD), lambda bi, i: (bi, i, 0))],
        compiler_params=_params(("parallel", "parallel")),
        name="kv_proj",
    )(t, an, w_ukv, g, g2, cs)


def _attn_kernel(*refs, layer, n_lat, tk):
    if n_lat:
        bounded_ref, q_ref, kc_ref, vc_ref, kl_ref, vl_ref, o_ref, s0_ref, s1_ref = refs
    else:
        bounded_ref, q_ref, kc_ref, vc_ref, o_ref = refs
    n_chunks = n_lat // tk if n_lat else 0

    def finish(acc):
        o_ref[0] = (acc[:, :V_DIM] * (1.0 / acc[:, V_DIM:V_DIM + 1])).astype(o_ref.dtype)

    @pl.when(bounded_ref[layer] != 0)
    def _():
        q = q_ref[0]
        acc = _dot(jnp.exp2(_dot_nt(q, kc_ref[0])).astype(BF16), vc_ref[0])
        for j in range(n_chunks):
            p = jnp.exp2(_dot_nt(q, kl_ref[0, j * tk:(j + 1) * tk, :])).astype(BF16)
            acc = acc + _dot(p, vl_ref[0, j * tk:(j + 1) * tk, :])
        finish(acc)

    @pl.when(bounded_ref[layer] == 0)
    def _():
        q = q_ref[0]
        tq = q.shape[0]

        def update(carry, s, v):
            m, acc = carry
            m_new = jnp.maximum(m, jnp.max(s, axis=-1, keepdims=True))
            p = jnp.exp2(s - m_new).astype(BF16)
            return m_new, jnp.exp2(m - m_new) * acc + _dot(p, v)

        carry = (jnp.full((tq, 1), -jnp.inf, F32), jnp.zeros((tq, HEAD_PAD), F32))
        if n_lat:
            slots = (s0_ref, s1_ref)

            def scores(j):
                slots[j % 2][...] = _dot_nt(q, kl_ref[0, j * tk:(j + 1) * tk, :])

            scores(0)
            carry = update(carry, _dot_nt(q, kc_ref[0]), vc_ref[0])
            for j in range(n_chunks):
                if j + 1 < n_chunks:
                    scores(j + 1)
                carry = update(carry, slots[j % 2][...], vl_ref[0, j * tk:(j + 1) * tk, :])
        else:
            carry = update(carry, _dot_nt(q, kc_ref[0]), vc_ref[0])
        finish(carry[1])


def _attention(bounded, layer, q, k_c, v_c, k_l=None, v_l=None, *, tq, tk):
    b, s, _ = q.shape
    n_ctx = k_c.shape[1]
    n_lat = 0 if k_l is None else k_l.shape[1]
    in_specs = [pl.BlockSpec(memory_space=pltpu.SMEM),
                pl.BlockSpec((1, tq, HEAD_PAD), lambda bi, h, i: (bi, i, h)),
                pl.BlockSpec((1, n_ctx, HEAD_PAD), lambda bi, h, i: (bi, 0, h)),
                pl.BlockSpec((1, n_ctx, HEAD_PAD), lambda bi, h, i: (bi, 0, h))]
    args = [bounded, q, k_c, v_c]
    scratch = []
    if n_lat:
        tk = min(tk, n_lat)
        assert n_lat % tk == 0
        in_specs += [pl.BlockSpec((1, n_lat, HEAD_PAD), lambda bi, h, i: (bi, 0, h)),
                     pl.BlockSpec((1, n_lat, HEAD_PAD), lambda bi, h, i: (bi, 0, h))]
        args += [k_l, v_l]
        scratch = [pltpu.VMEM((tq, tk), F32), pltpu.VMEM((tq, tk), F32)]
    return pl.pallas_call(
        functools.partial(_attn_kernel, layer=layer, n_lat=n_lat, tk=tk),
        out_shape=jax.ShapeDtypeStruct((b, s, ATT_W), BF16),
        grid=(b, N_HEADS, s // tq),
        in_specs=in_specs,
        out_specs=pl.BlockSpec((1, tq, V_DIM), lambda bi, h, i: (bi, i, h)),
        scratch_shapes=scratch,
        compiler_params=_params(("parallel", "parallel", "arbitrary")),
        name="attention",
    )(*args)


def _sgu_kernel(p_ref, gn_ref, w_ref, b_ref, o_ref, *, n_chunks):
    for c in range(n_chunks):
        rows = slice(c * SGU_CH, (c + 1) * SGU_CH)
        z = _gelu_tanh(p_ref[0, rows, :].astype(F32))
        u = z[:, :SGU_W]
        vn = _rms(z[:, SGU_W:], gn_ref[...]).astype(BF16)
        for g in range(SGU_G):
            cols = slice(g * SGU_CH, (g + 1) * SGU_CH)
            mixed = _dot(w_ref[g], vn[:, cols]) + b_ref[g]
            o_ref[0, rows, cols] = (u[:, cols] * mixed).astype(o_ref.dtype)


def _sgu(p, gn, w_s, b_b, layer, rows):
    b, s, _ = p.shape
    return pl.pallas_call(
        functools.partial(_sgu_kernel, n_chunks=rows // SGU_CH),
        out_shape=jax.ShapeDtypeStruct((b, s, SGU_W), BF16),
        grid=(b, s // rows),
        in_specs=[pl.BlockSpec((1, rows, 2 * SGU_W), lambda bi, i: (bi, i, 0)),
                  _layer_block(layer, 1, SGU_W),
                  _layer_block(layer, SGU_G, SGU_CH, SGU_CH),
                  _layer_block(layer, SGU_G, SGU_CH, SGU_CH)],
        out_specs=pl.BlockSpec((1, rows, SGU_W), lambda bi, i: (bi, i, 0)),
        compiler_params=_params(("parallel", "parallel")),
        name="sgu",
    )(p, gn, w_s, b_b)


def _prep_kernel(cq_ref, anq_ref, wq_ref, gq_ref, gq2_ref, t_ref, ankv_ref, wkv_ref, gk_ref, gk2_ref, cs_ref,
                 p_ref, gn_ref, ws_ref, bs_ref, q_ref, k_ref, v_ref, sgu_ref, *, n_chunks):
    _q_proj_kernel(cq_ref, anq_ref, wq_ref, gq_ref, gq2_ref, cs_ref, q_ref)
    _kv_proj_kernel(t_ref, ankv_ref, wkv_ref, gk_ref, gk2_ref, cs_ref, k_ref, v_ref)
    _sgu_kernel(p_ref, gn_ref, ws_ref, bs_ref, sgu_ref, n_chunks=n_chunks)


def _prep(cq, an_q, w_uq_p, gq, gq2, t, an_kv, w_ukv, gk, gk2, cs, p, gn, w_s, b_b, layer, tm):
    b, s, _ = cq.shape
    width = N_HEADS * HEAD_PAD
    row = lambda w: pl.BlockSpec((1, tm, w), lambda bi, i: (bi, i, 0))
    return pl.pallas_call(
        functools.partial(_prep_kernel, n_chunks=tm // SGU_CH),
        out_shape=[jax.ShapeDtypeStruct((b, s, width), BF16), jax.ShapeDtypeStruct((b, s, width), BF16),
                   jax.ShapeDtypeStruct((b, s, width), BF16), jax.ShapeDtypeStruct((b, s, SGU_W), BF16)],
        grid=(b, s // tm),
        in_specs=[row(Q_LORA), _layer_block(layer, 1, Q_LORA), _layer_block(layer, Q_LORA, N_HEADS * Q_HEAD_COLS),
                  _layer_block(layer, 1, NOPE), _layer_block(layer, 1, 2 * V7X_LANES),
                  row(t.shape[2]), _layer_block(layer, 1, KV_LORA), _layer_block(layer, KV_LORA, N_HEADS * 2 * NOPE),
                  _layer_block(layer, 1, NOPE), _layer_block(layer, 1, 2 * V7X_LANES),
                  pl.BlockSpec((tm, 2 * V7X_LANES), lambda bi, i: (i, 0)),
                  row(2 * SGU_W), _layer_block(layer, 1, SGU_W), _layer_block(layer, SGU_G, SGU_CH, SGU_CH),
                  _layer_block(layer, SGU_G, SGU_CH, SGU_CH)],
        out_specs=[row(width), row(width), row(width), row(SGU_W)],
        compiler_params=_params(("parallel", "parallel")),
        name="prep",
    )(cq, an_q, w_uq_p, gq, gq2, t, an_kv, w_ukv, gk, gk2, cs, p, gn, w_s, b_b)


def _ssd_direction(x_ref, dt_ref, bias_ref, alog_ref, tri_ref, exp_ref, state_ref, y_ref, *, bi, reverse):
    q = SSD_Q
    xs = x_ref[bi, :, :SSD_IN].astype(F32)
    dt = _softplus(dt_ref[bi] + bias_ref[...])
    a_dt = dt * (-jnp.exp(alog_ref[...]))
    tri = tri_ref[...]
    acs = sum(_dot(tri, part) for part in _split3(a_dt))
    acs_t = acs.T
    expand = exp_ref[...]
    acs_x = sum(_dot(part, expand) for part in _split3(acs))
    dt_hi = dt.astype(BF16)
    dt_x = _dot(dt_hi, expand) + _dot((dt - dt_hi.astype(F32)).astype(BF16), expand)
    last = acs_x[0:1, :] if reverse else acs_x[q - 1:q, :]
    xdt = xs * dt_x
    x_end = (xdt * jnp.exp(last - acs_x)).astype(BF16)
    decay_out = jnp.exp(acs_x)
    chunk_decay = jnp.exp(last)

    ti = lax.broadcasted_iota(jnp.int32, (q, q), 0)
    tj = lax.broadcasted_iota(jnp.int32, (q, q), 1)
    keep = (ti <= tj) if reverse else (ti >= tj)
    lane_blk = lax.shift_right_logical(lax.broadcasted_iota(jnp.int32, (q, GRP_W), 1), int(math.log2(SSD_P)))
    head0 = SSD_H if reverse else 0
    heads_per_group = SSD_H // SSD_G
    for g in range(SSD_G):
        bm = x_ref[bi, :, SSD_IN + g * SSD_N:SSD_IN + (g + 1) * SSD_N]
        cm = x_ref[bi, :, SSD_IN + (SSD_G + g) * SSD_N:SSD_IN + (SSD_G + g + 1) * SSD_N]
        cb = _dot_nt(cm, bm)
        cols = slice(g * GRP_W, (g + 1) * GRP_W)
        state = state_ref[g]
        y_off = _dot(cm, state.astype(BF16)) * decay_out[:, cols]
        state_ref[g] = state * chunk_decay[:, cols] + _dot(bm.astype(F32).T.astype(BF16), x_end[:, cols])
        blocks, stacked = [], []
        for hh in range(heads_per_group):
            hcol = head0 + g * heads_per_group + hh
            seg = acs[:, hcol:hcol + 1] - acs_t[hcol:hcol + 1, :]
            blocks.append((cb * jnp.exp(jnp.where(keep, seg, -jnp.inf))).astype(BF16))
            stacked.append(jnp.where(lane_blk == hh, xdt[:, cols], 0.0).astype(BF16))
        y = y_off + _dot(jnp.concatenate(blocks, axis=1), jnp.concatenate(stacked, axis=0))
        y_ref[bi, :, cols] = y.astype(y_ref.dtype)


def _ssd_kernel(xf_ref, dtf_ref, xb_ref, dtb_ref, bias_ref, alog_ref, tri_ref, exp_ref, init_ref,
                yf_ref, yb_ref, fin_ref, state_ref):
    s = pl.program_id(0)

    @pl.when(s == 0)
    def _():
        state_ref[...] = init_ref[...]

    for bi in range(xf_ref.shape[0]):
        _ssd_direction(xf_ref, dtf_ref, bias_ref, alog_ref, tri_ref.at[0], exp_ref.at[0], state_ref.at[bi, 0],
                       yf_ref, bi=bi, reverse=False)
        _ssd_direction(xb_ref, dtb_ref, bias_ref, alog_ref, tri_ref.at[1], exp_ref.at[1], state_ref.at[bi, 1],
                       yb_ref, bi=bi, reverse=True)

    @pl.when(s == pl.num_programs(0) - 1)
    def _():
        fin_ref[...] = state_ref[...]


def _ssd(xbc, dt, bias, alog, tri, expand, init, layer):
    b, s, _ = xbc.shape
    nc = s // SSD_Q

    def main_f(i): return (0, i, 0)
    def main_b(i): return (0, nc - 1 - i, 0)
    const3 = lambda i: (0, 0, 0)
    state_shape = (b, 2, SSD_G, SSD_N, GRP_W)
    state_block = pl.BlockSpec(state_shape, lambda i: (0, 0, 0, 0, 0))
    return pl.pallas_call(
        _ssd_kernel,
        out_shape=[jax.ShapeDtypeStruct((b, s, SSD_IN), BF16), jax.ShapeDtypeStruct((b, s, SSD_IN), BF16),
                   jax.ShapeDtypeStruct(state_shape, F32)],
        grid=(nc,),
        in_specs=[pl.BlockSpec((b, SSD_Q, XBC_W), main_f), pl.BlockSpec((b, SSD_Q, V7X_LANES), main_f),
                  pl.BlockSpec((b, SSD_Q, XBC_W), main_b), pl.BlockSpec((b, SSD_Q, V7X_LANES), main_b),
                  _layer_block(layer, 1, V7X_LANES), _layer_block(layer, 1, V7X_LANES),
                  pl.BlockSpec((2, SSD_Q, SSD_Q), const3), pl.BlockSpec((2, V7X_LANES, SSD_IN), const3),
                  state_block],
        out_specs=[pl.BlockSpec((b, SSD_Q, SSD_IN), main_f), pl.BlockSpec((b, SSD_Q, SSD_IN), main_b),
                   state_block],
        scratch_shapes=[pltpu.VMEM(state_shape, F32)],
        compiler_params=_params(("arbitrary",)),
        name="ssd",
    )(xbc, dt, xbc, dt, bias, alog, tri, expand, init)


def _out_proj_kernel(x_ref, mod_ref, att_ref, sgu_ref, yf_ref, yb_ref, xs_ref, z_ref,
                     ga_ref, gs_ref, d_ref, gy_ref, gf_ref, w_ref, o_ref, h_ref):
    att = _rms(att_ref[0].astype(F32), ga_ref[...]).astype(BF16)
    sgu = _rms(sgu_ref[0].astype(F32), gs_ref[...]).astype(BF16)
    y = yf_ref[0].astype(F32) + yb_ref[0].astype(F32) + d_ref[...] * xs_ref[0].astype(F32)
    ssd = _rms(y * _silu(z_ref[0].astype(F32)), gy_ref[...]).astype(BF16)
    mix = (_dot(att, w_ref[:ATT_W, :]) + _dot(sgu, w_ref[ATT_W:ATT_W + SGU_W, :])
           + _dot(ssd, w_ref[ATT_W + SGU_W:, :]))
    x_new = x_ref[0] + mod_ref[0, 2:3, :] * mix
    o_ref[0] = x_new
    h_ref[0] = (_rms(x_new, gf_ref[...]) * (1.0 + mod_ref[0, 4:5, :]) + mod_ref[0, 3:4, :]).astype(BF16)


def _out_proj(x, mods, mod_row, att, sgu, yf, yb, xs, z, ga, gs, dvec, gy, gf, w_out, layer, tm):
    b, s, d = x.shape
    row = lambda w: pl.BlockSpec((1, tm, w), lambda bi, i: (bi, i, 0))
    vec = lambda w: _layer_block(layer, 1, w)
    return pl.pallas_call(
        _out_proj_kernel,
        out_shape=[jax.ShapeDtypeStruct((b, s, d), F32), jax.ShapeDtypeStruct((b, s, d), BF16)],
        grid=(b, s // tm),
        in_specs=[row(d), _mod_block(layer, d, mod_row),
                  row(ATT_W), row(SGU_W), row(SSD_IN), row(SSD_IN), row(SSD_IN), row(SSD_IN),
                  vec(ATT_W), vec(SGU_W), vec(SSD_IN), vec(SSD_IN), vec(d),
                  pl.BlockSpec((None,) + w_out.shape[1:], lambda bi, i: (layer, 0, 0),
                               pipeline_mode=pl.Buffered(1))],
        out_specs=[row(d), row(d)],
        compiler_params=_params(("parallel", "parallel")),
        name="out_proj",
    )(x, mods, att, sgu, yf, yb, xs, z, ga, gs, dvec, gy, gf, w_out)


def _ffn_kernel(x_ref, h_ref, hp_ref, hn_ref, mod_ref, wg_ref, wu_ref, cw_ref, cb_ref, wd_ref, o_ref,
                halo_sc, acc_sc, *, seq_rows):
    i = pl.program_id(1)
    j = pl.program_id(2)
    hb = BF16_ROWS

    tm = h_ref.shape[1]

    @pl.when(j == 0)
    def _():
        halo_sc[0:tm, :] = h_ref[0]
        halo_sc[tm:tm + hb, :] = jnp.where(i > 0, hp_ref[0], jnp.zeros_like(hp_ref[0]))
        halo_sc[tm + hb:, :] = jnp.where(i < pl.num_programs(1) - 1, hn_ref[0], jnp.zeros_like(hn_ref[0]))
        acc_sc[...] = jnp.zeros_like(acc_sc)

    h = h_ref[0]
    gate_all = _dot(halo_sc[...], wg_ref[...])
    gate = gate_all[:tm]
    gate_halo = gate_all[tm:]
    row = lax.broadcasted_iota(jnp.int32, gate.shape, 0)
    up = jnp.where(row == 0, gate_halo[hb - 1:hb, :], pltpu.roll(gate, 1, axis=0))
    dn = jnp.where(row == tm - 1, gate_halo[hb:hb + 1, :], pltpu.roll(gate, tm - 1, axis=0))
    if seq_rows is not None:
        pos = lax.rem(row, seq_rows)
        up = jnp.where(pos == 0, 0.0, up)
        dn = jnp.where(pos == seq_rows - 1, 0.0, dn)
    conv = cb_ref[...] + cw_ref[0:1, :] * up + cw_ref[1:2, :] * gate + cw_ref[2:3, :] * dn
    act = (_silu(conv) * _dot(h, wu_ref[...])).astype(BF16)
    acc_sc[...] += _dot(act, wd_ref[...])

    @pl.when(j == pl.num_programs(2) - 1)
    def _():
        o_ref[0] = x_ref[0] + mod_ref[0, 5:6, :] * acc_sc[...]


def _ffn(x, h, mods, mod_row, w_gate, w_up, conv_w, conv_b, w_down, layer, tm, tf, pack_sequences=False):
    out_shape = x.shape
    seq_rows = None
    if pack_sequences:
        assert mod_row is not None
        seq_rows, d = x.shape[1], x.shape[2]
        x, h = x.reshape(1, -1, d), h.reshape(1, -1, d)
        tm = x.shape[1]
    b, s, d = x.shape
    f = w_down.shape[1]
    per = tm // BF16_ROWS
    last_blk = s // BF16_ROWS - 1
    assert f % tf == 0
    return pl.pallas_call(
        functools.partial(_ffn_kernel, seq_rows=seq_rows),
        out_shape=jax.ShapeDtypeStruct((b, s, d), F32),
        grid=(b, s // tm, f // tf),
        in_specs=[pl.BlockSpec((1, tm, d), lambda bi, i, j: (bi, i, 0)),
                  pl.BlockSpec((1, tm, d), lambda bi, i, j: (bi, i, 0)),
                  pl.BlockSpec((1, BF16_ROWS, d), lambda bi, i, j: (bi, jnp.maximum(i * per - 1, 0), 0)),
                  pl.BlockSpec((1, BF16_ROWS, d), lambda bi, i, j: (bi, jnp.minimum((i + 1) * per, last_blk), 0)),
                  _mod_block(layer, d, mod_row),
                  pl.BlockSpec((None, d, tf), lambda bi, i, j: (layer, 0, j)),
                  pl.BlockSpec((None, d, tf), lambda bi, i, j: (layer, 0, j)),
                  pl.BlockSpec((None, 3, tf), lambda bi, i, j: (layer, 0, j)),
                  pl.BlockSpec((None, 1, tf), lambda bi, i, j: (layer, 0, j)),
                  pl.BlockSpec((None, tf, d), lambda bi, i, j: (layer, j, 0))],
        out_specs=pl.BlockSpec((1, tm, d), lambda bi, i, j: (bi, i, 0)),
        scratch_shapes=[pltpu.VMEM((tm + 2 * BF16_ROWS, d), BF16), pltpu.VMEM((tm, d), F32)],
        compiler_params=_params(("parallel", "parallel", "arbitrary")),
        name="ffn",
    )(x, h, h, h, mods, w_gate, w_up, conv_w, conv_b, w_down).reshape(out_shape)


def _rope_tables(seq):
    rows = seq // GRID_COLS
    pairs = ROPE // 4
    freqs = ROPE_THETA ** (-jnp.arange(pairs, dtype=F32) / pairs)
    ar = jnp.arange(rows, dtype=F32)[:, None] * freqs
    ac = jnp.arange(GRID_COLS, dtype=F32)[:, None] * freqs
    by_row = lambda t: jnp.repeat(t, GRID_COLS, axis=0)
    by_col = lambda t: jnp.tile(t, (rows, 1))
    cr, sr, cc, sc = by_row(jnp.cos(ar)), by_row(jnp.sin(ar)), by_col(jnp.cos(ac)), by_col(jnp.sin(ac))
    cos = jnp.concatenate([cr, cr, cc, cc], axis=-1)
    sin = jnp.concatenate([-sr, sr, -sc, sc], axis=-1)
    zeros = jnp.zeros_like(cos)
    return jnp.concatenate([cos, zeros, sin, zeros], axis=-1)


def _pack_w_in(w):
    w = w.astype(BF16)
    off_kr, off_sgu = Q_LORA + KV_LORA, Q_LORA + KV_LORA + ROPE
    off_dt = off_sgu + 2 * SGU_W + SSD_IN + XBC_W
    kr = w[..., off_kr:off_sgu]
    dt = w[..., off_dt:]
    z64 = jnp.zeros(w.shape[:2] + (V7X_LANES - ROPE,), BF16)
    return jnp.concatenate([w[..., :off_kr], kr, z64, kr[..., _ROT_PERM], z64, w[..., off_sgu:off_dt], dt,
                            jnp.zeros(w.shape[:2] + (V7X_LANES - dt.shape[-1],), BF16)], axis=-1)


def _pack_w_uq(w):
    depth = w.shape[0]
    pairs = N_HEADS // 2
    w = w.astype(BF16).reshape(depth, Q_LORA, pairs, 2, QK)
    rot = w[..., NOPE:]
    z64 = jnp.zeros((depth, Q_LORA, pairs, 2, V7X_LANES - ROPE), BF16)
    both = lambda t: t.reshape(depth, Q_LORA, pairs, 2 * V7X_LANES)
    return jnp.concatenate([both(w[..., :NOPE]), both(jnp.concatenate([rot, z64], axis=-1)),
                            both(jnp.concatenate([rot[..., _ROT_PERM], z64], axis=-1))],
                           axis=-1).reshape(depth, Q_LORA, N_HEADS * Q_HEAD_COLS)


def _rot_gains(g):
    rot = g[:, NOPE:]
    z64 = jnp.zeros((g.shape[0], V7X_LANES - ROPE), g.dtype)
    return jnp.concatenate([rot, z64, rot[:, _ROT_PERM], z64], axis=-1)[:, None, :]


def _row_vectors(v, width=None):
    v = v.reshape(v.shape[0], 1, -1)
    return v if width is None else jnp.pad(v, ((0, 0), (0, 0), (0, width - v.shape[-1])))


def kernel(x, c, ctx, c_ctx, w_mod, b_mod, norm_mix, norm_ffn, w_in, q_a_norm, w_uq, kv_a_norm, w_ukv, q_norm, k_norm, attn_out_norm, sgu_norm, sgu_w, sgu_b, gmlp_out_norm, ssd_conv_w, ssd_conv_b, ssd_dt_bias, ssd_a_log, ssd_d, ssd_norm, w_out, ffn_w_gate, ffn_w_up, ffn_conv_w, ffn_conv_b, ffn_w_down):
    batch, seq, d = x.shape
    n_ctx = ctx.shape[1]
    depth = w_mod.shape[0]
    tm = min(512, seq)
    tm_ctx = min(256, n_ctx)
    tf = 512
    tk_att = 1024
    tq_att = 1024

    cc = jnp.concatenate([c, c_ctx[None], jnp.zeros((V7X_SUBLANES - batch - 1, d), F32)], axis=0)
    mods = jnp.pad(_mods(cc, w_mod, b_mod).reshape(depth, V7X_SUBLANES, 6, d),
                   ((0, 0), (0, 0), (0, V7X_SUBLANES - 6), (0, 0)))
    ctx_row = batch

    cs_lat = _rope_tables(seq)
    cs_ctx = jnp.concatenate([jnp.ones((n_ctx, ROPE), F32), jnp.zeros((n_ctx, 2 * V7X_LANES - ROPE), F32)], axis=-1)

    ti = np.arange(SSD_Q)
    tri = jnp.asarray(np.stack([ti[:, None] >= ti[None, :], ti[:, None] <= ti[None, :]]), BF16)
    lane_head = np.arange(SSD_IN) // SSD_P
    col = np.arange(V7X_LANES)
    expand = jnp.asarray(np.stack([col[:, None] == lane_head[None, :],
                                   col[:, None] == lane_head[None, :] + SSD_H]), BF16)

    w_in_p, w_uq_p = _pack_w_in(w_in), _pack_w_uq(w_uq)
    w_ukv_b, w_out_b = w_ukv.astype(BF16), w_out.astype(BF16)
    wg, wu, wd = ffn_w_gate.astype(BF16), ffn_w_up.astype(BF16), ffn_w_down.astype(BF16)

    g_mix, g_ffn = _row_vectors(norm_mix), _row_vectors(norm_ffn)
    an_q, an_kv = _row_vectors(q_a_norm), _row_vectors(kv_a_norm)
    gq1, gk1 = _row_vectors(q_norm[:, :NOPE]), _row_vectors(k_norm[:, :NOPE])
    gq2, gk2 = _rot_gains(q_norm), _rot_gains(k_norm)
    g_att, g_sgu_in, g_sgu_out = _row_vectors(attn_out_norm), _row_vectors(sgu_norm), _row_vectors(gmlp_out_norm)
    g_ssd = _row_vectors(ssd_norm)
    sgu_w_b = sgu_w.astype(BF16)
    sgu_b_b = jnp.broadcast_to(sgu_b[..., None], sgu_b.shape + (SGU_CH,))
    conv_b = _row_vectors(ssd_conv_b)
    bias, alog = _row_vectors(ssd_dt_bias, V7X_LANES), _row_vectors(ssd_a_log, V7X_LANES)
    dvec = _row_vectors(jnp.repeat(ssd_d, SSD_P, axis=1))
    fcb = _row_vectors(ffn_conv_b)
    score_bound = (QK * SM_SCALE * LOG2_E) * jnp.max(jnp.abs(q_norm), axis=1) * jnp.max(jnp.abs(k_norm), axis=1)
    bounded = (score_bound <= SCORE_BOUND_LIMIT).astype(jnp.int32)

    x_lat, x_ctx = x, ctx
    for l in range(depth):
        need_ctx = l < depth - 1

        def mixers(xs_in, mod_row, cs, t_rows, init_state, k_c=None, v_c=None, outputs=True):
            pq, pkv, psgu, pz, xbc, pdt = _in_proj(xs_in, mods, mod_row, g_mix, w_in_p, ssd_conv_w, conv_b, l, t_rows)
            yf, yb, fin = _ssd(xbc, pdt, bias, alog, tri, expand, init_state, l)
            if not outputs:
                k, v = _kv_proj(pkv, an_kv, w_ukv_b, l, gk1, gk2, cs, t_rows)
                return None, k, v, fin
            q, k, v, sgu = _prep(pq, an_q, w_uq_p, gq1, gq2, pkv, an_kv, w_ukv_b, gk1, gk2, cs,
                                 psgu, g_sgu_in, sgu_w_b, sgu_b_b, l, t_rows)
            if k_c is None:
                att = _attention(bounded, l, q, k, v, tq=t_rows, tk=tk_att)
            else:
                att = _attention(bounded, l, q, k_c, v_c, k, v, tq=min(tq_att, q.shape[1]), tk=tk_att)
            out = _out_proj(xs_in, mods, mod_row, att, sgu, yf, yb, xbc, pz, g_att, g_sgu_out, dvec, g_ssd, g_ffn,
                            w_out_b, l, t_rows)
            return out, k, v, fin

        zero_state = jnp.zeros((batch, 2, SSD_G, SSD_N, GRP_W), F32)
        ctx_mid, k_c, v_c, ctx_state = mixers(x_ctx, ctx_row, cs_ctx, tm_ctx, zero_state, outputs=need_ctx)
        (x_lat, h_lat), _, _, _ = mixers(x_lat, None, cs_lat, tm, ctx_state, k_c, v_c)
        x_lat = _ffn(x_lat, h_lat, mods, None, wg, wu, ffn_conv_w, fcb, wd, l, tm, tf)
        if need_ctx:
            x_ctx = _ffn(ctx_mid[0], ctx_mid[1], mods, ctx_row, wg, wu, ffn_conv_w, fcb, wd, l, tm_ctx, tf,
                         pack_sequences=True)
    return x_lat
```
